```python
import math
import jax, jax.numpy as jnp
from jax import lax
import numpy as np

D_MODEL = 2048
BATCH = 4
SEQ = 4096
DEPTH = 1
DEC_BATCH = 8
DEC_SEQ = 64
PAST_LEN = 1024

CHUNK = 64
A_HEADS = 8
A_DK = 128
A_DV = 128
A_QK = A_HEADS * A_DK
A_WIDTH = A_HEADS * A_DV
B_HEADS = 8
B_DH = 64
B_DV = 2 * B_DH
B_QK = B_HEADS * 2 * B_DH
B_WIDTH = B_HEADS * B_DV
MIX_WIDTH = A_WIDTH + B_WIDTH
IN_SPLITS = [A_QK, 2 * A_QK, 2 * A_QK + A_WIDTH, 2 * A_QK + 2 * A_WIDTH,
             2 * A_QK + 2 * A_WIDTH + B_QK, 2 * A_QK + 2 * A_WIDTH + 2 * B_QK]
IN_COLS = 2 * A_QK + 2 * A_WIDTH + 2 * B_QK + B_WIDTH
Q_BLOCK = 128
PEER_HEADS = 8
N_KEYS = 128
N_EXPERTS = N_KEYS * N_KEYS
PEER_TOPK = 16
PEER_DQ = 256
PEER_DHALF = PEER_DQ // 2
TOKEN_BLOCK = 128
LN_EPS = 1e-5
RMS_EPS = 1e-5
DEEPNORM_ALPHA = (2.0 * DEPTH) ** 0.25
DEEPNORM_BETA = (8.0 * DEPTH) ** -0.25

kernel_name = "hymba_hgrn2_diffattn_peer_stream"


def layer_norm(x, g, b):
    xf = x.astype(jnp.float32)
    mu = jnp.mean(xf, axis=-1, keepdims=True)
    var = jnp.mean(jnp.square(xf - mu), axis=-1, keepdims=True)
    return ((xf - mu) * lax.rsqrt(var + LN_EPS)).astype(x.dtype) * g + b


def rms_norm(x, g):
    xf = x.astype(jnp.float32)
    return (xf * lax.rsqrt(jnp.mean(xf * xf, axis=-1, keepdims=True) + RMS_EPS)) * g


def hgrn2_recurrence(q, k, log_f, v, s0):
    bsz, length, h, _ = q.shape
    dv = v.shape[-1]
    c = min(CHUNK, length)
    n = length // c

    def chunks(t):
        return jnp.moveaxis(t.astype(jnp.float32).reshape(bsz, n, c, h, t.shape[-1]), 1, 0)

    causal = jnp.tril(jnp.ones((c, c), dtype=bool))[None, :, :, None, None]

    def step(s, inp):
        qc, kc, lfc, vc = inp
        b = jnp.cumsum(lfc, axis=1)
        decay = jnp.exp(jnp.where(causal, b[:, :, None] - b[:, None, :], -jnp.inf))
        scores = jnp.einsum('bthk,bshk,btshk->bhts', qc, kc, decay)
        o = (jnp.einsum('bhts,bshv->bthv', scores, vc)
             + jnp.einsum('bthk,bhkv->bthv', qc * jnp.exp(b), s))
        b_end = b[:, -1]
        s_new = (jnp.exp(b_end)[..., None] * s
                 + jnp.einsum('bshk,bshv->bhkv', kc * jnp.exp(b_end[:, None] - b), vc))
        return s_new, o

    s_fin, o = lax.scan(step, s0.astype(jnp.float32),
                        (chunks(q), chunks(k), chunks(log_f), chunks(v)))
    return jnp.moveaxis(o, 0, 1).reshape(bsz, length, h, dv), s_fin


def diff_attend(q, k, v, qpos, kpos, lam):
    slopes = 2.0 ** (-8.0 * jnp.arange(1, B_HEADS + 1, dtype=jnp.float32) / B_HEADS)
    s = jnp.einsum('bthmd,bshmd->bhmts', q, k).astype(jnp.float32) * (B_DH ** -0.5)
    dist = jnp.abs(qpos[:, None] - kpos[None, :]).astype(jnp.float32)
    visible = (kpos[None, :] // CHUNK) <= (qpos[:, None] // CHUNK)
    bias = jnp.where(visible[None], -slopes[:, None, None] * dist[None], -jnp.inf)
    p = jax.nn.softmax(s + bias[None, :, None], axis=-1)
    w = p[:, :, 0] - lam * p[:, :, 1]
    return jnp.einsum('bhts,bshv->bthv', w.astype(v.dtype), v)


def diff_attention_prompt(q, k, v, lam):
    bsz, length = q.shape[0], q.shape[1]
    nq = length // Q_BLOCK
    qb = jnp.moveaxis(q.reshape(bsz, nq, Q_BLOCK, B_HEADS, 2, B_DH), 1, 0)
    kpos = jnp.arange(length)

    def block(args):
        qi, i = args
        qpos = i * Q_BLOCK + jnp.arange(Q_BLOCK)
        return diff_attend(qi, k, v, qpos, kpos, lam)

    out = lax.map(block, (qb, jnp.arange(nq)))
    return jnp.moveaxis(out, 0, 1).reshape(bsz, length, B_HEADS, B_DV)


def peer(x, wq, sub_keys, u, v):
    bsz, length, d = x.shape
    n = bsz * length
    pad = (-n) % TOKEN_BLOCK
    xb = jnp.pad(x.reshape(n, d), ((0, pad), (0, 0))).reshape(-1, TOKEN_BLOCK, d)

    def block(xt):
        q = (xt @ wq).reshape(TOKEN_BLOCK, PEER_HEADS, 2, PEER_DHALF)
        s = jnp.einsum('thcd,hckd->thck', q, sub_keys).astype(jnp.float32)
        sv, si = lax.top_k(s, PEER_TOPK)
        cand = (sv[:, :, 0, :, None] + sv[:, :, 1, None, :]).reshape(TOKEN_BLOCK, PEER_HEADS, -1)
        cid = (si[:, :, 0, :, None] * N_KEYS + si[:, :, 1, None, :]).reshape(TOKEN_BLOCK, PEER_HEADS, -1)
        cv, ci = lax.top_k(cand, PEER_TOPK)
        eidx = jnp.take_along_axis(cid, ci, axis=-1)
        g = jax.nn.softmax(cv, axis=-1)
        act = jax.nn.gelu(jnp.einsum('thkd,td->thk', u[eidx], xt).astype(jnp.float32), approximate=False)
        return jnp.einsum('thk,thkd->td', (g * act).astype(xt.dtype), v[eidx])

    y = lax.map(block, xb).reshape(-1, d)[:n]
    return y.reshape(bsz, length, d)


def trunk_layer(x, past_k, past_v, s0, lb, lam, lam_init, w_in, a_norm_g, b_norm_g, w_out,
                ln1_g, ln1_b, peer_wq, peer_sub_keys, peer_u, peer_v, ln2_g, ln2_b):
    bsz, length, _ = x.shape
    qa, fa, ia, ga, qb, kb, vb = jnp.split(jnp.einsum('bld,dc->blc', x, w_in), IN_SPLITS, axis=-1)

    f = lb + (1.0 - lb) * jax.nn.sigmoid(fa.astype(jnp.float32))
    a_heads = lambda t, dd: t.reshape(bsz, length, A_HEADS, dd)
    if s0 is None:
        s0 = jnp.zeros((bsz, A_HEADS, A_DK, A_DV), jnp.float32)
    o_a, s_new = hgrn2_recurrence(a_heads(qa, A_DK), a_heads(1.0 - f, A_DK),
                                  a_heads(jnp.log(f), A_DK), a_heads(ia, A_DV), s0)
    o_a = rms_norm(o_a, a_norm_g) * jax.nn.sigmoid(a_heads(ga, A_DV).astype(jnp.float32))

    q = qb.reshape(bsz, length, B_HEADS, 2, B_DH)
    k_new = kb.reshape(bsz, length, B_HEADS, 2 * B_DH)
    v_new = vb.reshape(bsz, length, B_HEADS, B_DV)
    if past_k is None:
        o_b = diff_attention_prompt(q, k_new.reshape(bsz, length, B_HEADS, 2, B_DH), v_new, lam)
    else:
        past = past_k.shape[1]
        k_all = jnp.concatenate([past_k.astype(k_new.dtype), k_new], axis=1)
        v_all = jnp.concatenate([past_v.astype(v_new.dtype), v_new], axis=1)
        qpos = past + jnp.arange(length)
        kpos = jnp.arange(past + length)
        o_b = diff_attend(q, k_all.reshape(bsz, past + length, B_HEADS, 2, B_DH), v_all, qpos, kpos, lam)
    o_b = rms_norm(o_b, b_norm_g) * (1.0 - lam_init)

    merged = jnp.concatenate([o_a.reshape(bsz, length, A_WIDTH),
                              o_b.reshape(bsz, length, B_WIDTH)], axis=-1).astype(x.dtype)
    x = layer_norm(DEEPNORM_ALPHA * x + merged @ w_out, ln1_g, ln1_b)
    x = layer_norm(DEEPNORM_ALPHA * x + peer(x, peer_wq, peer_sub_keys, peer_u, peer_v), ln2_g, ln2_b)
    return x, k_new, v_new, s_new


def setup_inputs(seed: int = 0) -> dict:
    key = jax.random.key(seed)
    ks = jax.random.split(key, 24)
    f32 = jnp.float32
    nrm = lambda k, shape, s: jax.random.normal(k, shape, f32) * s
    beta = DEEPNORM_BETA
    col_scale = jnp.concatenate([jnp.ones((2 * A_QK,), f32), jnp.full((A_WIDTH,), beta, f32),
                                 jnp.ones((A_WIDTH + 2 * B_QK,), f32), jnp.full((B_WIDTH,), beta, f32)])
    return {
        "x_prompt": nrm(ks[0], (BATCH, SEQ, D_MODEL), 1.0),
        "x_sample": nrm(ks[1], (DEC_BATCH, DEC_SEQ, D_MODEL), 1.0),
        "cache_k": nrm(ks[2], (DEPTH, DEC_BATCH, PAST_LEN, B_HEADS, 2 * B_DH), 1.0),
        "cache_v": nrm(ks[3], (DEPTH, DEC_BATCH, PAST_LEN, B_HEADS, B_DV), 0.5),
        "state_hgrn": nrm(ks[4], (DEPTH, DEC_BATCH, A_HEADS, A_DK, A_DV), 0.5),
        "w_in": nrm(ks[5], (DEPTH, D_MODEL, IN_COLS), D_MODEL ** -0.5) * col_scale,
        "hgrn_lb": nrm(ks[6], (DEPTH + 1, A_QK), 0.1),
        "hgrn_norm_g": 1.0 + nrm(ks[7], (DEPTH, A_DV), 0.02),
        "diff_lq1": nrm(ks[8], (DEPTH, B_DH), 0.1),
        "diff_lk1": nrm(ks[9], (DEPTH, B_DH), 0.1),
        "diff_lq2": nrm(ks[10], (DEPTH, B_DH), 0.1),
        "diff_lk2": nrm(ks[11], (DEPTH, B_DH), 0.1),
        "diff_norm_g": 1.0 + nrm(ks[12], (DEPTH, B_DV), 0.02),
        "w_out": nrm(ks[13], (DEPTH, MIX_WIDTH, D_MODEL), beta * MIX_WIDTH ** -0.5),
        "ln1_g": 1.0 + nrm(ks[14], (DEPTH, D_MODEL), 0.02),
        "ln1_b": nrm(ks[15], (DEPTH, D_MODEL), 0.02),
        "peer_wq": nrm(ks[16], (DEPTH, D_MODEL, PEER_HEADS * PEER_DQ), D_MODEL ** -0.5),
        "peer_sub_keys": nrm(ks[17], (DEPTH, PEER_HEADS, 2, N_KEYS, PEER_DHALF), PEER_DHALF ** -0.5),
        "peer_u": nrm(ks[18], (DEPTH, N_EXPERTS, D_MODEL), D_MODEL ** -0.5),
        "peer_v": nrm(ks[19], (DEPTH, N_EXPERTS, D_MODEL), beta * PEER_HEADS ** -0.5),
        "ln2_g": 1.0 + nrm(ks[20], (DEPTH, D_MODEL), 0.02),
        "ln2_b": nrm(ks[21], (DEPTH, D_MODEL), 0.02),
    }


def reference(x_prompt, x_sample, cache_k, cache_v, state_hgrn, w_in, hgrn_lb, hgrn_norm_g,
              diff_lq1, diff_lk1, diff_lq2, diff_lk2, diff_norm_g, w_out, ln1_g, ln1_b,
              peer_wq, peer_sub_keys, peer_u, peer_v, ln2_g, ln2_b):
    lower_bounds = jnp.cumsum(jax.nn.softmax(hgrn_lb.astype(jnp.float32), axis=0), axis=0)
    y_p, y_s = x_prompt, x_sample
    kp_l, vp_l, sp_l, ks_l, vs_l, ss_l = [], [], [], [], [], []
    for l in range(DEPTH):
        lam_init = 0.8 - 0.6 * math.exp(-0.3 * l)
        lam = (jnp.exp(jnp.sum(diff_lq1[l].astype(jnp.float32) * diff_lk1[l].astype(jnp.float32)))
               - jnp.exp(jnp.sum(diff_lq2[l].astype(jnp.float32) * diff_lk2[l].astype(jnp.float32)))
               + lam_init)
        shared = (lower_bounds[l], lam, lam_init, w_in[l], hgrn_norm_g[l], diff_norm_g[l], w_out[l],
                  ln1_g[l], ln1_b[l], peer_wq[l], peer_sub_keys[l], peer_u[l], peer_v[l], ln2_g[l], ln2_b[l])
        y_p, kp, vp, sp = trunk_layer(y_p, None, None, None, *shared)
        y_s, kn, vn, sn = trunk_layer(y_s, cache_k[l], cache_v[l], state_hgrn[l], *shared)
        kp_l.append(kp); vp_l.append(vp); sp_l.append(sp)
        ks_l.append(kn); vs_l.append(vn); ss_l.append(sn)
    return (y_p, y_s, jnp.stack(kp_l), jnp.stack(vp_l), jnp.stack(sp_l),
            jnp.stack(ks_l), jnp.stack(vs_l), jnp.stack(ss_l))
```

```python
import functools
import math

import jax
import jax.numpy as jnp
from jax import lax
from jax.experimental import pallas as pl
from jax.experimental.pallas import tpu as pltpu

D_MODEL = 2048
CHUNK = 64
HEADS = 8
HEAD_DIM = 128
WIDTH = HEADS * HEAD_DIM
N_GROUPS = 7
B_DH = 64
PEER_HEADS = 8
N_KEYS = 128
PEER_TOPK = 16
PEER_SLOTS = PEER_HEADS * PEER_TOPK
TOKEN_BLOCK = 128
LN_EPS = 1e-5
RMS_EPS = 1e-5
DEPTH = 1
DEEPNORM_ALPHA = (2.0 * DEPTH) ** 0.25
SUB = 16
LANE_GROUP = 512
VMEM_LIMIT = 48 * 1024 * 1024

_NT = (((1,), (1,)), ((), ()))
_TN = (((0,), (0,)), ((), ()))


def _bf(x):
    return x.astype(jnp.bfloat16)


def _dot(a, b):
    return jnp.dot(_bf(a), _bf(b), preferred_element_type=jnp.float32)


def _dot_nt(a, b):
    return lax.dot_general(_bf(a), _bf(b), _NT, preferred_element_type=jnp.float32)


def _dot_tn(a, b):
    return lax.dot_general(_bf(a), _bf(b), _TN, preferred_element_type=jnp.float32)


def _in_proj_kernel(x_ref, w_ref, o_ref):
    o_ref[0] = jnp.dot(_bf(x_ref[...]), w_ref[...], preferred_element_type=jnp.float32)


def _in_proj(x2d, w_bf16):
    n = x2d.shape[0]
    tm = 512
    assert n % tm == 0
    return pl.pallas_call(
        _in_proj_kernel,
        grid=(n // tm, N_GROUPS),
        in_specs=[pl.BlockSpec((tm, D_MODEL), lambda i, j: (i, 0)),
                  pl.BlockSpec((D_MODEL, WIDTH), lambda i, j: (0, j))],
        out_specs=pl.BlockSpec((1, tm, WIDTH), lambda i, j: (j, i, 0)),
        out_shape=jax.ShapeDtypeStruct((N_GROUPS, n, WIDTH), jnp.float32),
        compiler_params=pltpu.CompilerParams(
            dimension_semantics=("parallel", "arbitrary"), vmem_limit_bytes=VMEM_LIMIT),
        name="in_proj",
    )(x2d, w_bf16)


def _hgrn_kernel(q_ref, fa_ref, v_ref, ga_ref, s0_ref, lb_ref, g_ref, tri_ref,
                 o_ref, sfin_ref, st_ref, b_ref, kk_ref):
    c = pl.program_id(1)

    @pl.when(c == 0)
    def _():
        for h in range(HEADS):
            st_ref[h] = s0_ref[h].T

    lb = lb_ref[...]
    f = lb + (1.0 - lb) * jax.nn.sigmoid(fa_ref[...])
    lf = jnp.log(f)
    kk_ref[...] = 1.0 - f
    lf_hi = _bf(lf)
    lf_lo = _bf(lf - lf_hi.astype(jnp.float32))
    tri = tri_ref[...]
    b_ref[...] = (jnp.dot(tri, lf_hi, preferred_element_type=jnp.float32)
                  + jnp.dot(tri, lf_lo, preferred_element_type=jnp.float32))

    for h in range(HEADS):
        hs = slice(h * HEAD_DIM, (h + 1) * HEAD_DIM)
        bh = b_ref[:, hs]
        qh = q_ref[:, hs]
        kh = kk_ref[:, hs]
        vh = v_ref[:, hs]
        st = st_ref[h]
        o_h = _dot_nt(qh * jnp.exp(bh), st)
        parts = [o_h[0:SUB]]
        for i in range(1, CHUNK // SUB):
            r0 = i * SUB
            b_start = bh[r0 - 1:r0]
            q_i = qh[r0:r0 + SUB] * jnp.exp(bh[r0:r0 + SUB] - b_start)
            k_p = kh[0:r0] * jnp.exp(b_start - bh[0:r0])
            sc = _dot_nt(q_i, k_p)
            parts.append(o_h[r0:r0 + SUB] + _dot(sc, vh[0:r0]))
        o_ref[:, hs] = jnp.concatenate(parts, axis=0)
        b_end = bh[CHUNK - 1:CHUNK]
        kd = kh * jnp.exp(b_end - bh)
        st_ref[h] = st * jnp.exp(b_end) + _dot_tn(vh, kd)

    row = lax.broadcasted_iota(jnp.int32, (SUB, LANE_GROUP), 0)
    for i in range(CHUNK // SUB):
        r0 = i * SUB
        for g0 in range(0, WIDTH, LANE_GROUP):
            ls = slice(g0, g0 + LANE_GROUP)
            b_i = b_ref[r0:r0 + SUB, ls]
            q_i = q_ref[r0:r0 + SUB, ls]

            def body(s, acc, r0=r0, ls=ls, b_i=b_i, q_i=q_i):
                b_s = b_ref[pl.ds(r0 + s, 1), ls]
                k_s = kk_ref[pl.ds(r0 + s, 1), ls]
                v_s = v_ref[pl.ds(r0 + s, 1), ls]
                e = jnp.exp(jnp.where(row >= s, b_i - b_s, -jnp.inf))
                p = q_i * (k_s * e)
                cols = []
                for h in range(LANE_GROUP // HEAD_DIM):
                    hs = slice(h * HEAD_DIM, (h + 1) * HEAD_DIM)
                    w = jnp.sum(p[:, hs], axis=1, keepdims=True)
                    cols.append(w * v_s[:, hs])
                return acc + jnp.concatenate(cols, axis=1)

            acc = lax.fori_loop(0, SUB, body, jnp.zeros((SUB, LANE_GROUP), jnp.float32))
            o_ref[r0:r0 + SUB, ls] += acc

    gate = jax.nn.sigmoid(ga_ref[...])
    gain = g_ref[...]
    for h in range(HEADS):
        hs = slice(h * HEAD_DIM, (h + 1) * HEAD_DIM)
        o_h = o_ref[:, hs]
        ms = jnp.mean(o_h * o_h, axis=1, keepdims=True)
        o_ref[:, hs] = o_h * lax.rsqrt(ms + RMS_EPS) * gain * gate[:, hs]

    @pl.when(c == pl.num_programs(1) - 1)
    def _():
        for h in range(HEADS):
            sfin_ref[h] = st_ref[h].T


def _hgrn(z3, s0, lb, gain, batch, length):
    n_chunks = length // CHUNK
    z4 = z3.reshape(N_GROUPS, batch, length, WIDTH)
    tri = jnp.tril(jnp.ones((CHUNK, CHUNK), jnp.float32)).astype(jnp.bfloat16)

    def zspec(group):
        return pl.BlockSpec((None, None, CHUNK, WIDTH), lambda b, c, group=group: (group, b, c, 0))

    o, s_fin = pl.pallas_call(
        _hgrn_kernel,
        grid=(batch, n_chunks),
        in_specs=[zspec(0), zspec(1), zspec(2), zspec(3),
                  pl.BlockSpec((None, HEADS, HEAD_DIM, HEAD_DIM), lambda b, c: (b, 0, 0, 0)),
                  pl.BlockSpec((1, WIDTH), lambda b, c: (0, 0)),
                  pl.BlockSpec((1, HEAD_DIM), lambda b, c: (0, 0)),
                  pl.BlockSpec((CHUNK, CHUNK), lambda b, c: (0, 0))],
        out_specs=[pl.BlockSpec((None, CHUNK, WIDTH), lambda b, c: (b, c, 0)),
                   pl.BlockSpec((None, HEADS, HEAD_DIM, HEAD_DIM), lambda b, c: (b, 0, 0, 0))],
        out_shape=[jax.ShapeDtypeStruct((batch, length, WIDTH), jnp.float32),
                   jax.ShapeDtypeStruct((batch, HEADS, HEAD_DIM, HEAD_DIM), jnp.float32)],
        scratch_shapes=[pltpu.VMEM((HEADS, HEAD_DIM, HEAD_DIM), jnp.float32),
                        pltpu.VMEM((CHUNK, WIDTH), jnp.float32),
                        pltpu.VMEM((CHUNK, WIDTH), jnp.float32)],
        compiler_params=pltpu.CompilerParams(
            dimension_semantics=("parallel", "arbitrary"), vmem_limit_bytes=VMEM_LIMIT),
        name="hgrn2",
    )(z4, z4, z4, z4, s0, lb, gain, tri)
    return o.reshape(batch * length, WIDTH), s_fin


def _attn_kernel(par_ref, q_ref, k_ref, v_ref, g_ref, o_ref,
                 q1_ref, q2_ref, m_ref, l_ref, acc_ref, *, tq, tk, qpos0, s_valid, causal, lam_init):
    h = pl.program_id(1)
    qi = pl.program_id(2)
    ki = pl.program_id(3)
    nk = pl.num_programs(3)
    lam = par_ref[0]
    slope = par_ref[1 + h]

    @pl.when(ki == 0)
    def _():
        lane = lax.broadcasted_iota(jnp.int32, (tq, HEAD_DIM), 1)
        q = q_ref[...] * (B_DH ** -0.5)
        q1_ref[...] = _bf(jnp.where(lane < B_DH, q, 0.0))
        q2_ref[...] = _bf(jnp.where(lane >= B_DH, q, 0.0))
        m_ref[...] = jnp.full(m_ref.shape, -jnp.inf, jnp.float32)
        l_ref[...] = jnp.zeros(l_ref.shape, jnp.float32)
        acc_ref[...] = jnp.zeros(acc_ref.shape, jnp.float32)

    def step():
        kb = _bf(k_ref[...])
        vb = _bf(v_ref[...])
        qpos = qpos0 + qi * tq + lax.broadcasted_iota(jnp.int32, (tq, tk), 0)
        kpos = ki * tk + lax.broadcasted_iota(jnp.int32, (tq, tk), 1)
        bias = -slope * jnp.abs(qpos - kpos).astype(jnp.float32)
        visible = ((kpos // CHUNK) <= (qpos // CHUNK)) & (kpos < s_valid)
        bias = jnp.where(visible, bias, -jnp.inf)
        for m, qm_ref in enumerate((q1_ref, q2_ref)):
            s = lax.dot_general(qm_ref[...], kb, _NT, preferred_element_type=jnp.float32) + bias
            m_prev = m_ref[m]
            m_new = jnp.maximum(m_prev, jnp.max(s, axis=1, keepdims=True))
            alpha = jnp.exp(m_prev - m_new)
            p = jnp.exp(s - m_new)
            l_ref[m] = alpha * l_ref[m] + jnp.sum(p, axis=1, keepdims=True)
            acc_ref[m] = alpha * acc_ref[m] + jnp.dot(_bf(p), vb, preferred_element_type=jnp.float32)
            m_ref[m] = m_new

    if causal:
        last_q_chunk = (qpos0 + (qi + 1) * tq - 1) // CHUNK
        pl.when((ki * tk) // CHUNK <= last_q_chunk)(step)
    else:
        step()

    @pl.when(ki == nk - 1)
    def _():
        o = acc_ref[0] / l_ref[0] - lam * (acc_ref[1] / l_ref[1])
        ms = jnp.mean(o * o, axis=1, keepdims=True)
        o_ref[...] = o * lax.rsqrt(ms + RMS_EPS) * g_ref[...] * (1.0 - lam_init)


def _diff_attn(params, q, k, v, gain, *, tq, tk, qpos0, s_valid, causal, lam_init):
    batch, t, _ = q.shape
    s = k.shape[1]
    assert t % tq == 0 and s % tk == 0
    nq, nk = t // tq, s // tk

    if causal:
        def kv_map(b, h, qi, ki, par):
            last_q_chunk = (qpos0 + (qi + 1) * tq - 1) // CHUNK
            last = ((last_q_chunk + 1) * CHUNK - 1) // tk
            return (b, jnp.minimum(ki, last), h)
    else:
        def kv_map(b, h, qi, ki, par):
            return (b, ki, h)

    kern = functools.partial(_attn_kernel, tq=tq, tk=tk, qpos0=qpos0, s_valid=s_valid, causal=causal,
                             lam_init=lam_init)
    return pl.pallas_call(
        kern,
        grid_spec=pltpu.PrefetchScalarGridSpec(
            num_scalar_prefetch=1,
            grid=(batch, HEADS, nq, nk),
            in_specs=[pl.BlockSpec((None, tq, HEAD_DIM), lambda b, h, qi, ki, par: (b, qi, h)),
                      pl.BlockSpec((None, tk, HEAD_DIM), kv_map),
                      pl.BlockSpec((None, tk, HEAD_DIM), kv_map),
                      pl.BlockSpec((1, HEAD_DIM), lambda b, h, qi, ki, par: (0, 0))],
            out_specs=pl.BlockSpec((None, tq, HEAD_DIM), lambda b, h, qi, ki, par: (b, qi, h)),
            scratch_shapes=[pltpu.VMEM((tq, HEAD_DIM), jnp.bfloat16),
                            pltpu.VMEM((tq, HEAD_DIM), jnp.bfloat16),
                            pltpu.VMEM((2, tq, 1), jnp.float32),
                            pltpu.VMEM((2, tq, 1), jnp.float32),
                            pltpu.VMEM((2, tq, HEAD_DIM), jnp.float32)]),
        out_shape=jax.ShapeDtypeStruct((batch, t, WIDTH), jnp.float32),
        compiler_params=pltpu.CompilerParams(
            dimension_semantics=("parallel", "parallel", "parallel", "arbitrary"), vmem_limit_bytes=VMEM_LIMIT),
        name="diff_attn",
    )(params, q, k, v, gain)


def _layer_norm(y, g, b):
    mu = jnp.mean(y, axis=1, keepdims=True)
    d = y - mu
    var = jnp.mean(d * d, axis=1, keepdims=True)
    return d * lax.rsqrt(var + LN_EPS) * g + b


def _out_proj_kernel(oa_ref, ob_ref, x_ref, w_ref, g_ref, b_ref, o_ref):
    acc = jnp.dot(_bf(oa_ref[...]), w_ref[0:WIDTH, :], preferred_element_type=jnp.float32)
    acc += jnp.dot(_bf(ob_ref[...]), w_ref[WIDTH:2 * WIDTH, :], preferred_element_type=jnp.float32)
    o_ref[...] = _layer_norm(DEEPNORM_ALPHA * x_ref[...] + acc, g_ref[...], b_ref[...])


def _out_proj(oa, ob, x2d, w_bf16, g, b):
    n = x2d.shape[0]
    tm = 256
    assert n % tm == 0
    row = lambda i: (i, 0)
    const = lambda i: (0, 0)
    return pl.pallas_call(
        _out_proj_kernel,
        grid=(n // tm,),
        in_specs=[pl.BlockSpec((tm, WIDTH), row), pl.BlockSpec((tm, WIDTH), row),
                  pl.BlockSpec((tm, D_MODEL), row), pl.BlockSpec((2 * WIDTH, D_MODEL), const),
                  pl.BlockSpec((1, D_MODEL), const), pl.BlockSpec((1, D_MODEL), const)],
        out_specs=pl.BlockSpec((tm, D_MODEL), row),
        out_shape=jax.ShapeDtypeStruct((n, D_MODEL), jnp.float32),
        compiler_params=pltpu.CompilerParams(dimension_semantics=("parallel",), vmem_limit_bytes=VMEM_LIMIT),
        name="out_proj",
    )(oa, ob, x2d, w_bf16, g, b)


def _top16(vals, ids):
    r = vals.shape[0]
    pos = lax.broadcasted_iota(jnp.int32, vals.shape, 0).astype(jnp.float32)
    top_v, top_i = [], []
    for _ in range(PEER_TOPK):
        m = jnp.max(vals, axis=0, keepdims=True)
        first = jnp.min(jnp.where(vals == m, pos, float(r)), axis=0, keepdims=True)
        hit = pos == first
        top_v.append(m)
        top_i.append(jnp.max(jnp.where(hit, ids, -1.0), axis=0, keepdims=True))
        vals = jnp.where(hit, -jnp.inf, vals)
    return jnp.concatenate(top_v, axis=0), jnp.concatenate(top_i, axis=0)


def _peer_topk_kernel(x_ref, wq_ref, sk_ref, idx_ref, gate_ref):
    q = jnp.dot(_bf(x_ref[...]), wq_ref[...], preferred_element_type=jnp.float32)
    key_id = lax.broadcasted_iota(jnp.int32, (N_KEYS, TOKEN_BLOCK), 0).astype(jnp.float32)
    for h in range(PEER_HEADS):
        sv, si = [], []
        for c in range(2):
            col = (2 * h + c) * HEAD_DIM
            s_t = _dot_nt(sk_ref[2 * h + c], q[:, col:col + HEAD_DIM])
            v16, i16 = _top16(s_t, key_id)
            sv.append(v16)
            si.append(i16)
        cand = jnp.concatenate([sv[0][a:a + 1] + sv[1] for a in range(PEER_TOPK)], axis=0)
        cid = jnp.concatenate([si[0][a:a + 1] * N_KEYS + si[1] for a in range(PEER_TOPK)], axis=0)
        cv, eidx = _top16(cand, cid)
        e = jnp.exp(cv - cv[0:1])
        rows = slice(h * PEER_TOPK, (h + 1) * PEER_TOPK)
        gate_ref[0, rows, :] = e / jnp.sum(e, axis=0, keepdims=True)
        idx_ref[0, rows, :] = eidx.astype(jnp.int32)


def _peer_topk(x2d, wq_bf16, sk_bf16):
    n = x2d.shape[0]
    nb = n // TOKEN_BLOCK
    return pl.pallas_call(
        _peer_topk_kernel,
        grid=(nb,),
        in_specs=[pl.BlockSpec((TOKEN_BLOCK, D_MODEL), lambda i: (i, 0)),
                  pl.BlockSpec((D_MODEL, D_MODEL), lambda i: (0, 0)),
                  pl.BlockSpec((2 * PEER_HEADS, N_KEYS, HEAD_DIM), lambda i: (0, 0, 0))],
        out_specs=[pl.BlockSpec((1, PEER_SLOTS, TOKEN_BLOCK), lambda i: (i, 0, 0)),
                   pl.BlockSpec((1, PEER_SLOTS, TOKEN_BLOCK), lambda i: (i, 0, 0))],
        out_shape=[jax.ShapeDtypeStruct((nb, PEER_SLOTS, TOKEN_BLOCK), jnp.int32),
                   jax.ShapeDtypeStruct((nb, PEER_SLOTS, TOKEN_BLOCK), jnp.float32)],
        compiler_params=pltpu.CompilerParams(dimension_semantics=("parallel",), vmem_limit_bytes=VMEM_LIMIT),
        name="peer_topk",
    )(x2d, wq_bf16, sk_bf16)


PEER_TB = 16


def _peer_mix_kernel(idx_ref, gate_ref, x_ref, u_hbm, v_hbm, g_ref, b_ref, o_ref, ubuf, vbuf, sem):
    def row_copies(tok, j, slot):
        e = idx_ref[0, 0, tok * PEER_SLOTS + j]
        return (pltpu.make_async_copy(u_hbm.at[pl.ds(e, 1)], ubuf.at[slot, pl.ds(j, 1)], sem.at[0, slot]),
                pltpu.make_async_copy(v_hbm.at[pl.ds(e, 1)], vbuf.at[slot, pl.ds(j, 1)], sem.at[1, slot]))

    def start_token(tok, slot):
        def body(j, carry):
            cu, cv = row_copies(tok, j, slot)
            cu.start()
            cv.start()
            return carry
        lax.fori_loop(0, PEER_SLOTS, body, 0, unroll=8)

    def wait_token(tok, slot):
        def body(j, carry):
            cu, cv = row_copies(tok, j, slot)
            cu.wait()
            cv.wait()
            return carry
        lax.fori_loop(0, PEER_SLOTS, body, 0, unroll=8)

    start_token(0, 0)
    for t in range(PEER_TB):
        slot = t % 2
        if t + 1 < PEER_TB:
            start_token(t + 1, 1 - slot)
        wait_token(t, slot)
        x_t = x_ref[t:t + 1, :]
        act = jnp.sum(ubuf[slot] * x_t, axis=1, keepdims=True)
        act = 0.5 * act * (1.0 + lax.erf(act * (2.0 ** -0.5)))
        w = gate_ref[0, :, t:t + 1] * act
        y = jnp.sum(vbuf[slot] * w, axis=0, keepdims=True)
        o_ref[t:t + 1, :] = _layer_norm(DEEPNORM_ALPHA * x_t + y, g_ref[...], b_ref[...])


def _peer_mix(idx, gate, x2d, u, v, g, b):
    n = x2d.shape[0]
    steps = n // PEER_TB
    idx_tm = jnp.swapaxes(idx, 1, 2).reshape(steps, 1, PEER_TB * PEER_SLOTS)
    gate_tm = jnp.swapaxes(jnp.swapaxes(gate, 1, 2).reshape(steps, PEER_TB, PEER_SLOTS), 1, 2)
    return pl.pallas_call(
        _peer_mix_kernel,
        grid=(steps,),
        in_specs=[pl.BlockSpec((1, 1, PEER_TB * PEER_SLOTS), lambda i: (i, 0, 0), memory_space=pltpu.SMEM),
                  pl.BlockSpec((1, PEER_SLOTS, PEER_TB), lambda i: (i, 0, 0)),
                  pl.BlockSpec((PEER_TB, D_MODEL), lambda i: (i, 0)),
                  pl.BlockSpec(memory_space=pl.ANY),
                  pl.BlockSpec(memory_space=pl.ANY),
                  pl.BlockSpec((1, D_MODEL), lambda i: (0, 0)),
                  pl.BlockSpec((1, D_MODEL), lambda i: (0, 0))],
        out_specs=pl.BlockSpec((PEER_TB, D_MODEL), lambda i: (i, 0)),
        out_shape=jax.ShapeDtypeStruct((n, D_MODEL), jnp.float32),
        scratch_shapes=[pltpu.VMEM((2, PEER_SLOTS, D_MODEL), jnp.float32),
                        pltpu.VMEM((2, PEER_SLOTS, D_MODEL), jnp.float32),
                        pltpu.SemaphoreType.DMA((2, 2))],
        compiler_params=pltpu.CompilerParams(dimension_semantics=("arbitrary",), vmem_limit_bytes=VMEM_LIMIT),
        name="peer_mix",
    )(idx_tm, gate_tm, x2d, u, v, g, b)


def _trunk_layer(x, past_k, past_v, s0, lb, attn_params, lam_init, w_in, a_gain, b_gain, w_out,
                 ln1_g, ln1_b, wq, sub_keys, u, v, ln2_g, ln2_b):
    batch, length, _ = x.shape
    n = batch * length
    x2d = x.reshape(n, D_MODEL)
    z3 = _in_proj(x2d, w_in)
    k_new = z3[5].reshape(batch, length, HEADS, HEAD_DIM)
    v_new = z3[6].reshape(batch, length, HEADS, HEAD_DIM)

    if s0 is None:
        s0 = jnp.zeros((batch, HEADS, HEAD_DIM, HEAD_DIM), jnp.float32)
    o_a, s_new = _hgrn(z3, s0, lb, a_gain, batch, length)

    q = z3[4].reshape(batch, length, WIDTH)
    if past_k is None:
        o_b = _diff_attn(attn_params, q, z3[5].reshape(batch, length, WIDTH), z3[6].reshape(batch, length, WIDTH),
                         b_gain, tq=256, tk=256, qpos0=0, s_valid=length, causal=True, lam_init=lam_init)
    else:
        past = past_k.shape[1]
        total = past + length
        padded = -(-total // HEAD_DIM) * HEAD_DIM
        pad = jnp.zeros((batch, padded - total, WIDTH), jnp.float32)
        k_all = jnp.concatenate([past_k.reshape(batch, past, WIDTH), z3[5].reshape(batch, length, WIDTH), pad], axis=1)
        v_all = jnp.concatenate([past_v.reshape(batch, past, WIDTH), z3[6].reshape(batch, length, WIDTH), pad], axis=1)
        o_b = _diff_attn(attn_params, q, k_all, v_all, b_gain, tq=length, tk=padded, qpos0=past, s_valid=total,
                         causal=False, lam_init=lam_init)

    x1 = _out_proj(o_a, o_b.reshape(n, WIDTH), x2d, w_out, ln1_g, ln1_b)
    idx, gate = _peer_topk(x1, wq, sub_keys)
    y = _peer_mix(idx, gate, x1, u, v, ln2_g, ln2_b)
    return y.reshape(batch, length, D_MODEL), k_new, v_new, s_new


def kernel(x_prompt, x_sample, cache_k, cache_v, state_hgrn, w_in, hgrn_lb, hgrn_norm_g, diff_lq1, diff_lk1,
           diff_lq2, diff_lk2, diff_norm_g, w_out, ln1_g, ln1_b, peer_wq, peer_sub_keys, peer_u, peer_v,
           ln2_g, ln2_b):
    f32 = jnp.float32
    lower_bounds = jnp.cumsum(jax.nn.softmax(hgrn_lb.astype(f32), axis=0), axis=0)
    slopes = 2.0 ** (-8.0 * jnp.arange(1, HEADS + 1, dtype=f32) / HEADS)
    y_p, y_s = x_prompt, x_sample
    outs = [[] for _ in range(6)]
    for l in range(DEPTH):
        lam_init = 0.8 - 0.6 * math.exp(-0.3 * l)
        lam = (jnp.exp(jnp.sum(diff_lq1[l].astype(f32) * diff_lk1[l].astype(f32)))
               - jnp.exp(jnp.sum(diff_lq2[l].astype(f32) * diff_lk2[l].astype(f32))) + lam_init)
        attn_params = jnp.concatenate([lam.reshape(1), slopes])
        shared = (lower_bounds[l].reshape(1, WIDTH), attn_params, lam_init, _bf(w_in[l]),
                  hgrn_norm_g[l].reshape(1, HEAD_DIM), diff_norm_g[l].reshape(1, HEAD_DIM), _bf(w_out[l]),
                  ln1_g[l].reshape(1, D_MODEL), ln1_b[l].reshape(1, D_MODEL), _bf(peer_wq[l]),
                  _bf(peer_sub_keys[l].reshape(2 * PEER_HEADS, N_KEYS, HEAD_DIM)), peer_u[l], peer_v[l],
                  ln2_g[l].reshape(1, D_MODEL), ln2_b[l].reshape(1, D_MODEL))
        y_p, kp, vp, sp = _trunk_layer(y_p, None, None, None, *shared)
        y_s, kn, vn, sn = _trunk_layer(y_s, cache_k[l], cache_v[l], state_hgrn[l], *shared)
        for lst, val in zip(outs, (kp, vp, sp, kn, vn, sn)):
            lst.append(val)
    return (y_p, y_s) + tuple(jnp.stack(o) for o in outs)
```

```python
import functools
import math

import jax
import jax.numpy as jnp
from jax import lax
from jax.experimental import pallas as pl
from jax.experimental.pallas import tpu as pltpu

D_MODEL = 2048
CHUNK = 64
HEADS = 8
HEAD_DIM = 128
WIDTH = HEADS * HEAD_DIM
N_GROUPS = 7
B_DH = 64
PEER_HEADS = 8
N_KEYS = 128
PEER_TOPK = 16
PEER_SLOTS = PEER_HEADS * PEER_TOPK
TOKEN_BLOCK = 128
LN_EPS = 1e-5
RMS_EPS = 1e-5
DEPTH = 1
DEEPNORM_ALPHA = (2.0 * DEPTH) ** 0.25
ROWS, LANES = 8, 128
SUB = 16
LANE_GROUP = 512
VMEM_LIMIT = 48 * 1024 * 1024

_NT = (((1,), (1,)), ((), ()))
_TN = (((0,), (0,)), ((), ()))


def _bf(x):
    return x.astype(jnp.bfloat16)


def _dot(a, b):
    return jnp.dot(_bf(a), _bf(b), preferred_element_type=jnp.float32)


def _dot_nt(a, b):
    return lax.dot_general(_bf(a), _bf(b), _NT, preferred_element_type=jnp.float32)


def _dot_tn(a, b):
    return lax.dot_general(_bf(a), _bf(b), _TN, preferred_element_type=jnp.float32)


def _in_proj_kernel(x_ref, w_ref, o_ref):
    o_ref[0] = jnp.dot(_bf(x_ref[...]), w_ref[...], preferred_element_type=jnp.float32)


def _in_proj(x2d, w_bf16):
    n = x2d.shape[0]
    tm = 512
    assert n % tm == 0
    return pl.pallas_call(
        _in_proj_kernel,
        grid=(n // tm, N_GROUPS),
        in_specs=[pl.BlockSpec((tm, D_MODEL), lambda i, j: (i, 0)),
                  pl.BlockSpec((D_MODEL, WIDTH), lambda i, j: (0, j))],
        out_specs=pl.BlockSpec((1, tm, WIDTH), lambda i, j: (j, i, 0)),
        out_shape=jax.ShapeDtypeStruct((N_GROUPS, n, WIDTH), jnp.float32),
        compiler_params=pltpu.CompilerParams(
            dimension_semantics=("parallel", "arbitrary"), vmem_limit_bytes=VMEM_LIMIT),
        name="in_proj",
    )(x2d, w_bf16)


def _hgrn_kernel(q_ref, fa_ref, v_ref, ga_ref, s0_ref, lb_ref, g_ref, tri_ref,
                 o_ref, sfin_ref, st_ref, b_ref, kk_ref, *, n_chunks):
    c = pl.program_id(1)

    @pl.when(c == 0)
    def _():
        for h in range(HEADS):
            st_ref[h] = s0_ref[h].T

    lb = lb_ref[...]
    f = lb + (1.0 - lb) * jax.nn.sigmoid(fa_ref[...])
    lf = jnp.log(f)
    kk_ref[...] = 1.0 - f
    lf_hi = _bf(lf)
    lf_lo = _bf(lf - lf_hi.astype(jnp.float32))
    tri = tri_ref[...]
    b_ref[...] = (jnp.dot(tri, lf_hi, preferred_element_type=jnp.float32)
                  + jnp.dot(tri, lf_lo, preferred_element_type=jnp.float32))

    for h in range(HEADS):
        hs = slice(h * HEAD_DIM, (h + 1) * HEAD_DIM)
        bh = b_ref[:, hs]
        qh = q_ref[:, hs]
        kh = kk_ref[:, hs]
        vh = v_ref[:, hs]
        st = st_ref[h]
        o_h = _dot_nt(qh * jnp.exp(bh), st)
        parts = [o_h[0:SUB]]
        for i in range(1, CHUNK // SUB):
            r0 = i * SUB
            b_start = bh[r0 - 1:r0]
            q_i = qh[r0:r0 + SUB] * jnp.exp(bh[r0:r0 + SUB] - b_start)
            k_p = kh[0:r0] * jnp.exp(b_start - bh[0:r0])
            sc = _dot_nt(q_i, k_p)
            parts.append(o_h[r0:r0 + SUB] + _dot(sc, vh[0:r0]))
        o_ref[:, hs] = jnp.concatenate(parts, axis=0)
        b_end = bh[CHUNK - 1:CHUNK]
        kd = kh * jnp.exp(b_end - bh)
        st_ref[h] = st * jnp.exp(b_end) + _dot_tn(vh, kd)

    row = lax.broadcasted_iota(jnp.int32, (SUB, LANE_GROUP), 0)
    for i in range(CHUNK // SUB):
        r0 = i * SUB
        for g0 in range(0, WIDTH, LANE_GROUP):
            ls = slice(g0, g0 + LANE_GROUP)
            b_i = b_ref[r0:r0 + SUB, ls]
            q_i = q_ref[r0:r0 + SUB, ls]

            def body(s, acc, r0=r0, ls=ls, b_i=b_i, q_i=q_i):
                b_s = b_ref[pl.ds(r0 + s, 1), ls]
                k_s = kk_ref[pl.ds(r0 + s, 1), ls]
                v_s = v_ref[pl.ds(r0 + s, 1), ls]
                e = jnp.exp(jnp.where(row >= s, b_i - b_s, -jnp.inf))
                p = q_i * (k_s * e)
                cols = []
                for h in range(LANE_GROUP // HEAD_DIM):
                    hs = slice(h * HEAD_DIM, (h + 1) * HEAD_DIM)
                    w = jnp.sum(p[:, hs], axis=1, keepdims=True)
                    cols.append(w * v_s[:, hs])
                return acc + jnp.concatenate(cols, axis=1)

            acc = lax.fori_loop(0, SUB, body, jnp.zeros((SUB, LANE_GROUP), jnp.float32), unroll=True)
            o_ref[r0:r0 + SUB, ls] += acc

    gate = jax.nn.sigmoid(ga_ref[...])
    gain = g_ref[...]
    for h in range(HEADS):
        hs = slice(h * HEAD_DIM, (h + 1) * HEAD_DIM)
        o_h = o_ref[:, hs]
        ms = jnp.mean(o_h * o_h, axis=1, keepdims=True)
        o_ref[:, hs] = o_h * lax.rsqrt(ms + RMS_EPS) * gain * gate[:, hs]

    @pl.when(c == n_chunks - 1)
    def _():
        for h in range(HEADS):
            sfin_ref[h] = st_ref[h].T


def _hgrn(z3, s0, lb, gain, batch, length):
    n_chunks = length // CHUNK
    z4 = z3.reshape(N_GROUPS, batch, length, WIDTH)
    tri = jnp.tril(jnp.ones((CHUNK, CHUNK), jnp.float32)).astype(jnp.bfloat16)

    def zspec(group):
        return pl.BlockSpec((None, None, CHUNK, WIDTH), lambda b, c, group=group: (group, b, c, 0))

    o, s_fin = pl.pallas_call(
        functools.partial(_hgrn_kernel, n_chunks=n_chunks),
        grid=(batch, n_chunks),
        in_specs=[zspec(0), zspec(1), zspec(2), zspec(3),
                  pl.BlockSpec((None, HEADS, HEAD_DIM, HEAD_DIM), lambda b, c: (b, 0, 0, 0)),
                  pl.BlockSpec((1, WIDTH), lambda b, c: (0, 0)),
                  pl.BlockSpec((1, HEAD_DIM), lambda b, c: (0, 0)),
                  pl.BlockSpec((CHUNK, CHUNK), lambda b, c: (0, 0))],
        out_specs=[pl.BlockSpec((None, CHUNK, WIDTH), lambda b, c: (b, c, 0)),
                   pl.BlockSpec((None, HEADS, HEAD_DIM, HEAD_DIM), lambda b, c: (b, 0, 0, 0))],
        out_shape=[jax.ShapeDtypeStruct((batch, length, WIDTH), jnp.float32),
                   jax.ShapeDtypeStruct((batch, HEADS, HEAD_DIM, HEAD_DIM), jnp.float32)],
        scratch_shapes=[pltpu.VMEM((HEADS, HEAD_DIM, HEAD_DIM), jnp.float32),
                        pltpu.VMEM((CHUNK, WIDTH), jnp.float32),
                        pltpu.VMEM((CHUNK, WIDTH), jnp.float32)],
        compiler_params=pltpu.CompilerParams(
            dimension_semantics=("parallel", "arbitrary"), vmem_limit_bytes=VMEM_LIMIT),
        name="hgrn2",
    )(z4, z4, z4, z4, s0, lb, gain, tri)
    return o.reshape(batch * length, WIDTH), s_fin


def _attn_init(q_ref, q1_ref, q2_ref, m_ref, l_ref, acc_ref):
    lane = lax.broadcasted_iota(jnp.int32, q_ref.shape, 1)
    q = q_ref[...] * (B_DH ** -0.5)
    q1_ref[...] = _bf(jnp.where(lane < B_DH, q, 0.0))
    q2_ref[...] = _bf(jnp.where(lane >= B_DH, q, 0.0))
    m_ref[...] = jnp.full(m_ref.shape, -jnp.inf, jnp.float32)
    l_ref[...] = jnp.zeros(l_ref.shape, jnp.float32)
    acc_ref[...] = jnp.zeros(acc_ref.shape, jnp.float32)


def _attn_update(bias, kb, vb, q1_ref, q2_ref, m_ref, l_ref, acc_ref):
    for m, qm_ref in enumerate((q1_ref, q2_ref)):
        s = lax.dot_general(qm_ref[...], kb, _NT, preferred_element_type=jnp.float32) + bias
        m_prev = m_ref[m]
        m_new = jnp.maximum(m_prev, jnp.max(s, axis=1, keepdims=True))
        alpha = jnp.exp(m_prev - m_new)
        p = jnp.exp(s - m_new)
        l_ref[m] = alpha * l_ref[m] + jnp.sum(p, axis=1, keepdims=True)
        acc_ref[m] = alpha * acc_ref[m] + jnp.dot(_bf(p), vb, preferred_element_type=jnp.float32)
        m_ref[m] = m_new


def _attn_finish(lam, lam_init, g_ref, o_ref, l_ref, acc_ref):
    o = acc_ref[0] / l_ref[0] - lam * (acc_ref[1] / l_ref[1])
    ms = jnp.mean(o * o, axis=1, keepdims=True)
    o_ref[...] = o * lax.rsqrt(ms + RMS_EPS) * g_ref[...] * (1.0 - lam_init)


def _attn_prompt_kernel(qt_ref, kt_ref, par_ref, q_ref, k_ref, v_ref, pat_ref, g_ref, o_ref,
                        q1_ref, q2_ref, m_ref, l_ref, acc_ref, *, t, lam_init):
    h = pl.program_id(1)
    pair = pl.program_id(2)
    qi = qt_ref[pair]
    ki = kt_ref[pair]
    slope = par_ref[1 + h]

    @pl.when(ki == 0)
    def _():
        _attn_init(q_ref, q1_ref, q2_ref, m_ref, l_ref, acc_ref)

    kb = _bf(k_ref[...])
    vb = _bf(v_ref[...])

    @pl.when(ki < qi)
    def _():
        col = lax.broadcasted_iota(jnp.int32, (1, t), 1)
        bias = slope * ((ki - qi) * t + col).astype(jnp.float32)
        _attn_update(bias, kb, vb, q1_ref, q2_ref, m_ref, l_ref, acc_ref)

    @pl.when(ki == qi)
    def _():
        _attn_update(slope * pat_ref[...], kb, vb, q1_ref, q2_ref, m_ref, l_ref, acc_ref)
        _attn_finish(par_ref[0], lam_init, g_ref, o_ref, l_ref, acc_ref)


def _attn_cached_kernel(par_ref, q_ref, k_ref, v_ref, g_ref, o_ref,
                        q1_ref, q2_ref, m_ref, l_ref, acc_ref, *, qpos0, s_valid, lam_init):
    tq, tk = q_ref.shape[0], k_ref.shape[0]
    slope = par_ref[1 + pl.program_id(1)]
    _attn_init(q_ref, q1_ref, q2_ref, m_ref, l_ref, acc_ref)
    qpos = qpos0 + lax.broadcasted_iota(jnp.int32, (tq, tk), 0)
    kpos = lax.broadcasted_iota(jnp.int32, (tq, tk), 1)
    visible = ((kpos // CHUNK) <= (qpos // CHUNK)) & (kpos < s_valid)
    bias = jnp.where(visible, -slope * jnp.abs(qpos - kpos).astype(jnp.float32), -jnp.inf)
    _attn_update(bias, _bf(k_ref[...]), _bf(v_ref[...]), q1_ref, q2_ref, m_ref, l_ref, acc_ref)
    _attn_finish(par_ref[0], lam_init, g_ref, o_ref, l_ref, acc_ref)


def _attn_scratch(tq):
    return [pltpu.VMEM((tq, HEAD_DIM), jnp.bfloat16),
            pltpu.VMEM((tq, HEAD_DIM), jnp.bfloat16),
            pltpu.VMEM((2, tq, 1), jnp.float32),
            pltpu.VMEM((2, tq, 1), jnp.float32),
            pltpu.VMEM((2, tq, HEAD_DIM), jnp.float32)]


ATTN_BLOCK = 512


def _diff_attn_prompt(params, q, k, v, gain, *, lam_init):
    batch, length, _ = q.shape
    t = ATTN_BLOCK
    assert length % t == 0 and t % CHUNK == 0
    nb = length // t
    pairs = [(qi, ki) for qi in range(nb) for ki in range(qi + 1)]
    qt = jnp.array([p[0] for p in pairs], jnp.int32)
    kt = jnp.array([p[1] for p in pairs], jnp.int32)
    r = lax.broadcasted_iota(jnp.int32, (t, t), 0)
    c = lax.broadcasted_iota(jnp.int32, (t, t), 1)
    pattern = jnp.where(c // CHUNK <= r // CHUNK, jnp.minimum(c, 2 * r - c).astype(jnp.float32), -jnp.inf)

    q_map = lambda b, h, p, qt, kt, par: (b, qt[p], h)
    k_map = lambda b, h, p, qt, kt, par: (b, kt[p], h)
    return pl.pallas_call(
        functools.partial(_attn_prompt_kernel, t=t, lam_init=lam_init),
        grid_spec=pltpu.PrefetchScalarGridSpec(
            num_scalar_prefetch=3,
            grid=(batch, HEADS, len(pairs)),
            in_specs=[pl.BlockSpec((None, t, HEAD_DIM), q_map),
                      pl.BlockSpec((None, t, HEAD_DIM), k_map),
                      pl.BlockSpec((None, t, HEAD_DIM), k_map),
                      pl.BlockSpec((t, t), lambda b, h, p, qt, kt, par: (0, 0)),
                      pl.BlockSpec((1, HEAD_DIM), lambda b, h, p, qt, kt, par: (0, 0))],
            out_specs=pl.BlockSpec((None, t, HEAD_DIM), q_map),
            scratch_shapes=_attn_scratch(t)),
        out_shape=jax.ShapeDtypeStruct((batch, length, WIDTH), jnp.float32),
        compiler_params=pltpu.CompilerParams(
            dimension_semantics=("parallel", "parallel", "arbitrary"), vmem_limit_bytes=VMEM_LIMIT),
        name="diff_attn_prompt",
    )(qt, kt, params, q, k, v, pattern, gain)


def _diff_attn_cached(params, q, k, v, gain, *, qpos0, s_valid, lam_init):
    batch, t, _ = q.shape
    s = k.shape[1]
    return pl.pallas_call(
        functools.partial(_attn_cached_kernel, qpos0=qpos0, s_valid=s_valid, lam_init=lam_init),
        grid_spec=pltpu.PrefetchScalarGridSpec(
            num_scalar_prefetch=1,
            grid=(batch, HEADS),
            in_specs=[pl.BlockSpec((None, t, HEAD_DIM), lambda b, h, par: (b, 0, h)),
                      pl.BlockSpec((None, s, HEAD_DIM), lambda b, h, par: (b, 0, h)),
                      pl.BlockSpec((None, s, HEAD_DIM), lambda b, h, par: (b, 0, h)),
                      pl.BlockSpec((1, HEAD_DIM), lambda b, h, par: (0, 0))],
            out_specs=pl.BlockSpec((None, t, HEAD_DIM), lambda b, h, par: (b, 0, h)),
            scratch_shapes=_attn_scratch(t)),
        out_shape=jax.ShapeDtypeStruct((batch, t, WIDTH), jnp.float32),
        compiler_params=pltpu.CompilerParams(
            dimension_semantics=("parallel", "parallel"), vmem_limit_bytes=VMEM_LIMIT),
        name="diff_attn_cached",
    )(params, q, k, v, gain)


def _layer_norm(y, g, b):
    mu = jnp.mean(y, axis=1, keepdims=True)
    d = y - mu
    var = jnp.mean(d * d, axis=1, keepdims=True)
    return d * lax.rsqrt(var + LN_EPS) * g + b


def _out_proj_kernel(oa_ref, ob_ref, x_ref, w_ref, g_ref, b_ref, o_ref):
    acc = jnp.dot(_bf(oa_ref[...]), w_ref[0:WIDTH, :], preferred_element_type=jnp.float32)
    acc += jnp.dot(_bf(ob_ref[...]), w_ref[WIDTH:2 * WIDTH, :], preferred_element_type=jnp.float32)
    o_ref[...] = _layer_norm(DEEPNORM_ALPHA * x_ref[...] + acc, g_ref[...], b_ref[...])


def _out_proj(oa, ob, x2d, w_bf16, g, b):
    n = x2d.shape[0]
    tm = 256
    assert n % tm == 0
    row = lambda i: (i, 0)
    const = lambda i: (0, 0)
    return pl.pallas_call(
        _out_proj_kernel,
        grid=(n // tm,),
        in_specs=[pl.BlockSpec((tm, WIDTH), row), pl.BlockSpec((tm, WIDTH), row),
                  pl.BlockSpec((tm, D_MODEL), row), pl.BlockSpec((2 * WIDTH, D_MODEL), const),
                  pl.BlockSpec((1, D_MODEL), const), pl.BlockSpec((1, D_MODEL), const)],
        out_specs=pl.BlockSpec((tm, D_MODEL), row),
        out_shape=jax.ShapeDtypeStruct((n, D_MODEL), jnp.float32),
        compiler_params=pltpu.CompilerParams(dimension_semantics=("parallel",), vmem_limit_bytes=VMEM_LIMIT),
        name="out_proj",
    )(oa, ob, x2d, w_bf16, g, b)


def _top16(vals, ids):
    r = vals.shape[0]
    pos = lax.broadcasted_iota(jnp.int32, vals.shape, 0).astype(jnp.float32)
    top_v, top_i = [], []
    for _ in range(PEER_TOPK):
        m = jnp.max(vals, axis=0, keepdims=True)
        first = jnp.min(jnp.where(vals == m, pos, float(r)), axis=0, keepdims=True)
        hit = pos == first
        top_v.append(m)
        top_i.append(jnp.max(jnp.where(hit, ids, -1.0), axis=0, keepdims=True))
        vals = jnp.where(hit, -jnp.inf, vals)
    return jnp.concatenate(top_v, axis=0), jnp.concatenate(top_i, axis=0)


def _peer_topk_kernel(x_ref, wq_ref, sk_ref, idx_ref, gate_ref):
    q = jnp.dot(_bf(x_ref[...]), wq_ref[...], preferred_element_type=jnp.float32)
    key_id = lax.broadcasted_iota(jnp.int32, (N_KEYS, TOKEN_BLOCK), 0).astype(jnp.float32)
    for h in range(PEER_HEADS):
        sv, si = [], []
        for c in range(2):
            col = (2 * h + c) * HEAD_DIM
            s_t = _dot_nt(sk_ref[2 * h + c], q[:, col:col + HEAD_DIM])
            v16, i16 = _top16(s_t, key_id)
            sv.append(v16)
            si.append(i16)
        cand = jnp.concatenate([sv[0][a:a + 1] + sv[1] for a in range(PEER_TOPK)], axis=0)
        cid = jnp.concatenate([si[0][a:a + 1] * N_KEYS + si[1] for a in range(PEER_TOPK)], axis=0)
        cv, eidx = _top16(cand, cid)
        e = jnp.exp(cv - cv[0:1])
        rows = slice(h * PEER_TOPK, (h + 1) * PEER_TOPK)
        gate_ref[0, rows, :] = e / jnp.sum(e, axis=0, keepdims=True)
        idx_ref[0, rows, :] = eidx.astype(jnp.int32)


def _peer_topk(x2d, wq_bf16, sk_bf16):
    n = x2d.shape[0]
    nb = n // TOKEN_BLOCK
    return pl.pallas_call(
        _peer_topk_kernel,
        grid=(nb,),
        in_specs=[pl.BlockSpec((TOKEN_BLOCK, D_MODEL), lambda i: (i, 0)),
                  pl.BlockSpec((D_MODEL, D_MODEL), lambda i: (0, 0)),
                  pl.BlockSpec((2 * PEER_HEADS, N_KEYS, HEAD_DIM), lambda i: (0, 0, 0))],
        out_specs=[pl.BlockSpec((1, PEER_SLOTS, TOKEN_BLOCK), lambda i: (i, 0, 0)),
                   pl.BlockSpec((1, PEER_SLOTS, TOKEN_BLOCK), lambda i: (i, 0, 0))],
        out_shape=[jax.ShapeDtypeStruct((nb, PEER_SLOTS, TOKEN_BLOCK), jnp.int32),
                   jax.ShapeDtypeStruct((nb, PEER_SLOTS, TOKEN_BLOCK), jnp.float32)],
        compiler_params=pltpu.CompilerParams(dimension_semantics=("parallel",), vmem_limit_bytes=VMEM_LIMIT),
        name="peer_topk",
    )(x2d, wq_bf16, sk_bf16)


PEER_TB = 16
PEER_AHEAD = 8


def _pack_tables(u, v):
    ub = lax.bitcast_convert_type(_bf(u), jnp.uint16).astype(jnp.uint32)
    vb = lax.bitcast_convert_type(_bf(v), jnp.uint16).astype(jnp.uint32)
    return ((ub << 16) | vb).reshape(u.shape[0], 1, u.shape[1])


def _peer_mix_kernel(idx_ref, idx_next_ref, gate_ref, x_ref, tab_hbm, g_ref, b_ref, o_ref, buf, y_ref, sem, *,
                     steps):
    i = pl.program_id(0)

    def start_token(ids_ref, tok):
        for j in range(PEER_SLOTS):
            e = ids_ref[0, 0, tok * PEER_SLOTS + j]
            pltpu.make_async_copy(tab_hbm.at[e], buf.at[tok, pl.ds(j, 1)], sem.at[tok]).start()

    def wait_token(tok):
        pltpu.make_async_copy(tab_hbm.at[pl.ds(0, PEER_SLOTS), 0], buf.at[tok], sem.at[tok]).wait()

    @pl.when(i == 0)
    def _():
        for t in range(PEER_AHEAD):
            start_token(idx_ref, t)

    for t in range(PEER_TB):
        nxt = t + PEER_AHEAD
        if nxt < PEER_TB:
            start_token(idx_ref, nxt)
        else:
            start_token(idx_next_ref, nxt - PEER_TB)
        wait_token(t)
        x_b = jnp.broadcast_to(x_ref[t:t + 1, :], (ROWS, D_MODEL))
        parts = []
        for r0 in range(0, PEER_SLOTS, ROWS):
            words = buf[t, r0:r0 + ROWS, :]
            p = lax.bitcast_convert_type(words & jnp.uint32(0xFFFF0000), jnp.float32) * x_b
            acc = p[:, 0:LANES]
            for c0 in range(LANES, D_MODEL, LANES):
                acc = acc + p[:, c0:c0 + LANES]
            parts.append(acc)
        act = jnp.sum(jnp.concatenate(parts, axis=0), axis=1, keepdims=True)
        act = 0.5 * act * (1.0 + lax.erf(act * (2.0 ** -0.5)))
        w = gate_ref[0, :, t:t + 1] * act
        y_acc = jnp.zeros((ROWS, D_MODEL), jnp.float32)
        for r0 in range(0, PEER_SLOTS, ROWS):
            words = buf[t, r0:r0 + ROWS, :]
            y_acc = y_acc + lax.bitcast_convert_type(words << 16, jnp.float32) * w[r0:r0 + ROWS]
        y_ref[t:t + 1, :] = jnp.sum(y_acc, axis=0, keepdims=True)

    o_ref[...] = _layer_norm(DEEPNORM_ALPHA * x_ref[...] + y_ref[...], g_ref[...], b_ref[...])

    @pl.when(i == steps - 1)
    def _():
        for t in range(PEER_AHEAD):
            wait_token(t)


def _peer_mix(idx, gate, x2d, table, g, b):
    n = x2d.shape[0]
    steps = n // PEER_TB
    assert n % PEER_TB == 0 and PEER_AHEAD < PEER_TB
    idx_tm = jnp.swapaxes(idx, 1, 2).reshape(steps, 1, PEER_TB * PEER_SLOTS)
    gate_tm = jnp.swapaxes(jnp.swapaxes(gate, 1, 2).reshape(steps, PEER_TB, PEER_SLOTS), 1, 2)
    ids_block = (1, 1, PEER_TB * PEER_SLOTS)
    return pl.pallas_call(
        functools.partial(_peer_mix_kernel, steps=steps),
        grid=(steps,),
        in_specs=[pl.BlockSpec(ids_block, lambda i: (i, 0, 0), memory_space=pltpu.SMEM),
                  pl.BlockSpec(ids_block, lambda i: (jnp.minimum(i + 1, steps - 1), 0, 0), memory_space=pltpu.SMEM),
                  pl.BlockSpec((1, PEER_SLOTS, PEER_TB), lambda i: (i, 0, 0)),
                  pl.BlockSpec((PEER_TB, D_MODEL), lambda i: (i, 0)),
                  pl.BlockSpec(memory_space=pl.ANY),
                  pl.BlockSpec((1, D_MODEL), lambda i: (0, 0)),
                  pl.BlockSpec((1, D_MODEL), lambda i: (0, 0))],
        out_specs=pl.BlockSpec((PEER_TB, D_MODEL), lambda i: (i, 0)),
        out_shape=jax.ShapeDtypeStruct((n, D_MODEL), jnp.float32),
        scratch_shapes=[pltpu.VMEM((PEER_TB, PEER_SLOTS, D_MODEL), jnp.uint32),
                        pltpu.VMEM((PEER_TB, D_MODEL), jnp.float32),
                        pltpu.SemaphoreType.DMA((PEER_TB,))],
        compiler_params=pltpu.CompilerParams(dimension_semantics=("arbitrary",), vmem_limit_bytes=VMEM_LIMIT),
        name="peer_mix",
    )(idx_tm, idx_tm, gate_tm, x2d, table, g, b)


def _trunk_layer(x, past_k, past_v, s0, lb, attn_params, lam_init, w_in, a_gain, b_gain, w_out,
                 ln1_g, ln1_b, wq, sub_keys, table, ln2_g, ln2_b):
    batch, length, _ = x.shape
    n = batch * length
    x2d = x.reshape(n, D_MODEL)
    z3 = _in_proj(x2d, w_in)
    k_new = z3[5].reshape(batch, length, HEADS, HEAD_DIM)
    v_new = z3[6].reshape(batch, length, HEADS, HEAD_DIM)

    if s0 is None:
        s0 = jnp.zeros((batch, HEADS, HEAD_DIM, HEAD_DIM), jnp.float32)
    o_a, s_new = _hgrn(z3, s0, lb, a_gain, batch, length)

    q = z3[4].reshape(batch, length, WIDTH)
    if past_k is None:
        o_b = _diff_attn_prompt(attn_params, q, z3[5].reshape(batch, length, WIDTH),
                                z3[6].reshape(batch, length, WIDTH), b_gain, lam_init=lam_init)
    else:
        past = past_k.shape[1]
        total = past + length
        padded = -(-total // HEAD_DIM) * HEAD_DIM
        pad = jnp.zeros((batch, padded - total, WIDTH), jnp.float32)
        k_all = jnp.concatenate([past_k.reshape(batch, past, WIDTH), z3[5].reshape(batch, length, WIDTH), pad], axis=1)
        v_all = jnp.concatenate([past_v.reshape(batch, past, WIDTH), z3[6].reshape(batch, length, WIDTH), pad], axis=1)
        o_b = _diff_attn_cached(attn_params, q, k_all, v_all, b_gain, qpos0=past, s_valid=total, lam_init=lam_init)

    x1 = _out_proj(o_a, o_b.reshape(n, WIDTH), x2d, w_out, ln1_g, ln1_b)
    idx, gate = _peer_topk(x1, wq, sub_keys)
    y = _peer_mix(idx, gate, x1, table, ln2_g, ln2_b)
    return y.reshape(batch, length, D_MODEL), k_new, v_new, s_new


def kernel(x_prompt, x_sample, cache_k, cache_v, state_hgrn, w_in, hgrn_lb, hgrn_norm_g, diff_lq1, diff_lk1,
           diff_lq2, diff_lk2, diff_norm_g, w_out, ln1_g, ln1_b, peer_wq, peer_sub_keys, peer_u, peer_v,
           ln2_g, ln2_b):
    f32 = jnp.float32
    lower_bounds = jnp.cumsum(jax.nn.softmax(hgrn_lb.astype(f32), axis=0), axis=0)
    slopes = 2.0 ** (-8.0 * jnp.arange(1, HEADS + 1, dtype=f32) / HEADS)
    y_p, y_s = x_prompt, x_sample
    outs = [[] for _ in range(6)]
    for l in range(DEPTH):
        lam_init = 0.8 - 0.6 * math.exp(-0.3 * l)
        lam = (jnp.exp(jnp.sum(diff_lq1[l].astype(f32) * diff_lk1[l].astype(f32)))
               - jnp.exp(jnp.sum(diff_lq2[l].astype(f32) * diff_lk2[l].astype(f32))) + lam_init)
        attn_params = jnp.concatenate([lam.reshape(1), slopes])
        shared = (lower_bounds[l].reshape(1, WIDTH), attn_params, lam_init, _bf(w_in[l]),
                  hgrn_norm_g[l].reshape(1, HEAD_DIM), diff_norm_g[l].reshape(1, HEAD_DIM), _bf(w_out[l]),
                  ln1_g[l].reshape(1, D_MODEL), ln1_b[l].reshape(1, D_MODEL), _bf(peer_wq[l]),
                  _bf(peer_sub_keys[l].reshape(2 * PEER_HEADS, N_KEYS, HEAD_DIM)), _pack_tables(peer_u[l], peer_v[l]),
                  ln2_g[l].reshape(1, D_MODEL), ln2_b[l].reshape(1, D_MODEL))
        y_p, kp, vp, sp = _trunk_layer(y_p, None, None, None, *shared)
        y_s, kn, vn, sn = _trunk_layer(y_s, cache_k[l], cache_v[l], state_hgrn[l], *shared)
        for lst, val in zip(outs, (kp, vp, sp, kn, vn, sn)):
            lst.append(val)
    return (y_p, y_s) + tuple(jnp.stack(o) for o in outs)
```

```python
import functools
import math

import jax
import jax.numpy as jnp
from jax import lax
from jax.experimental import pallas as pl
from jax.experimental.pallas import tpu as pltpu

D_MODEL = 2048
CHUNK = 64
HEADS = 8
HEAD_DIM = 128
WIDTH = HEADS * HEAD_DIM
N_GROUPS = 7
B_DH = 64
PEER_HEADS = 8
N_KEYS = 128
PEER_TOPK = 16
PEER_SLOTS = PEER_HEADS * PEER_TOPK
TOKEN_BLOCK = 128
LN_EPS = 1e-5
RMS_EPS = 1e-5
DEPTH = 1
DEEPNORM_ALPHA = (2.0 * DEPTH) ** 0.25
ROWS, LANES = 8, 128
SUB = 16
LANE_GROUP = 512
VMEM_LIMIT = 48 * 1024 * 1024

_NT = (((1,), (1,)), ((), ()))
_TN = (((0,), (0,)), ((), ()))


def _bf(x):
    return x.astype(jnp.bfloat16)


def _dot(a, b):
    return jnp.dot(_bf(a), _bf(b), preferred_element_type=jnp.float32)


def _dot_nt(a, b):
    return lax.dot_general(_bf(a), _bf(b), _NT, preferred_element_type=jnp.float32)


def _dot_tn(a, b):
    return lax.dot_general(_bf(a), _bf(b), _TN, preferred_element_type=jnp.float32)


N_MIX_GROUPS = 5


def _in_proj_kernel(x_ref, w_ref, z_ref, k_ref, v_ref):
    j = pl.program_id(1)
    z = jnp.dot(_bf(x_ref[...]), w_ref[...], preferred_element_type=jnp.float32)

    @pl.when(j < N_MIX_GROUPS)
    def _():
        z_ref[...] = z

    @pl.when(j == N_MIX_GROUPS)
    def _():
        k_ref[...] = z

    @pl.when(j == N_MIX_GROUPS + 1)
    def _():
        v_ref[...] = z


def _in_proj(x2d, w_bf16):
    n = x2d.shape[0]
    tm = 512
    assert n % tm == 0
    return pl.pallas_call(
        _in_proj_kernel,
        grid=(n // tm, N_GROUPS),
        in_specs=[pl.BlockSpec((tm, D_MODEL), lambda i, j: (i, 0)),
                  pl.BlockSpec((D_MODEL, WIDTH), lambda i, j: (0, j))],
        out_specs=[pl.BlockSpec((None, tm, WIDTH), lambda i, j: (jnp.minimum(j, N_MIX_GROUPS - 1), i, 0)),
                   pl.BlockSpec((tm, WIDTH), lambda i, j: (i, 0)),
                   pl.BlockSpec((tm, WIDTH), lambda i, j: (i, 0))],
        out_shape=[jax.ShapeDtypeStruct((N_MIX_GROUPS, n, WIDTH), jnp.float32),
                   jax.ShapeDtypeStruct((n, WIDTH), jnp.float32),
                   jax.ShapeDtypeStruct((n, WIDTH), jnp.float32)],
        compiler_params=pltpu.CompilerParams(
            dimension_semantics=("parallel", "arbitrary"), vmem_limit_bytes=VMEM_LIMIT),
        name="in_proj",
    )(x2d, w_bf16)


def _hgrn_kernel(q_ref, fa_ref, v_ref, ga_ref, s0_ref, lb_ref, g_ref, tri_ref,
                 o_ref, sfin_ref, st_ref, b_ref, kk_ref, *, n_chunks):
    c = pl.program_id(1)

    @pl.when(c == 0)
    def _():
        for h in range(HEADS):
            st_ref[h] = s0_ref[h].T

    lb = lb_ref[...]
    f = lb + (1.0 - lb) * jax.nn.sigmoid(fa_ref[...])
    lf = jnp.log(f)
    kk_ref[...] = 1.0 - f
    lf_hi = _bf(lf)
    lf_lo = _bf(lf - lf_hi.astype(jnp.float32))
    tri = tri_ref[...]
    b_ref[...] = (jnp.dot(tri, lf_hi, preferred_element_type=jnp.float32)
                  + jnp.dot(tri, lf_lo, preferred_element_type=jnp.float32))

    for h in range(HEADS):
        hs = slice(h * HEAD_DIM, (h + 1) * HEAD_DIM)
        bh = b_ref[:, hs]
        qh = q_ref[:, hs]
        kh = kk_ref[:, hs]
        vh = v_ref[:, hs]
        st = st_ref[h]
        o_h = _dot_nt(qh * jnp.exp(bh), st)
        parts = [o_h[0:SUB]]
        for i in range(1, CHUNK // SUB):
            r0 = i * SUB
            b_start = bh[r0 - 1:r0]
            q_i = qh[r0:r0 + SUB] * jnp.exp(bh[r0:r0 + SUB] - b_start)
            k_p = kh[0:r0] * jnp.exp(b_start - bh[0:r0])
            sc = _dot_nt(q_i, k_p)
            parts.append(o_h[r0:r0 + SUB] + _dot(sc, vh[0:r0]))
        o_ref[:, hs] = jnp.concatenate(parts, axis=0)
        b_end = bh[CHUNK - 1:CHUNK]
        kd = kh * jnp.exp(b_end - bh)
        st_ref[h] = st * jnp.exp(b_end) + _dot_tn(vh, kd)

    row = lax.broadcasted_iota(jnp.int32, (SUB, LANE_GROUP), 0)
    for i in range(CHUNK // SUB):
        r0 = i * SUB
        for g0 in range(0, WIDTH, LANE_GROUP):
            ls = slice(g0, g0 + LANE_GROUP)
            b_i = b_ref[r0:r0 + SUB, ls]
            q_i = q_ref[r0:r0 + SUB, ls]

            def body(s, acc, r0=r0, ls=ls, b_i=b_i, q_i=q_i):
                b_s = b_ref[pl.ds(r0 + s, 1), ls]
                k_s = kk_ref[pl.ds(r0 + s, 1), ls]
                v_s = v_ref[pl.ds(r0 + s, 1), ls]
                e = jnp.exp(jnp.where(row >= s, b_i - b_s, -jnp.inf))
                p = q_i * (k_s * e)
                cols = []
                for h in range(LANE_GROUP // HEAD_DIM):
                    hs = slice(h * HEAD_DIM, (h + 1) * HEAD_DIM)
                    w = jnp.sum(p[:, hs], axis=1, keepdims=True)
                    cols.append(w * v_s[:, hs])
                return acc + jnp.concatenate(cols, axis=1)

            acc = lax.fori_loop(0, SUB, body, jnp.zeros((SUB, LANE_GROUP), jnp.float32), unroll=True)
            o_ref[r0:r0 + SUB, ls] += acc

    gate = jax.nn.sigmoid(ga_ref[...])
    gain = g_ref[...]
    for h in range(HEADS):
        hs = slice(h * HEAD_DIM, (h + 1) * HEAD_DIM)
        o_h = o_ref[:, hs]
        ms = jnp.mean(o_h * o_h, axis=1, keepdims=True)
        o_ref[:, hs] = o_h * lax.rsqrt(ms + RMS_EPS) * gain * gate[:, hs]

    @pl.when(c == n_chunks - 1)
    def _():
        for h in range(HEADS):
            sfin_ref[h] = st_ref[h].T


def _hgrn(z4, s0, lb, gain):
    _, batch, length, _ = z4.shape
    n_chunks = length // CHUNK
    tri = jnp.tril(jnp.ones((CHUNK, CHUNK), jnp.float32)).astype(jnp.bfloat16)

    def zspec(group):
        return pl.BlockSpec((None, None, CHUNK, WIDTH), lambda b, c, group=group: (group, b, c, 0))

    o, s_fin = pl.pallas_call(
        functools.partial(_hgrn_kernel, n_chunks=n_chunks),
        grid=(batch, n_chunks),
        in_specs=[zspec(0), zspec(1), zspec(2), zspec(3),
                  pl.BlockSpec((None, HEADS, HEAD_DIM, HEAD_DIM), lambda b, c: (b, 0, 0, 0)),
                  pl.BlockSpec((1, WIDTH), lambda b, c: (0, 0)),
                  pl.BlockSpec((1, HEAD_DIM), lambda b, c: (0, 0)),
                  pl.BlockSpec((CHUNK, CHUNK), lambda b, c: (0, 0))],
        out_specs=[pl.BlockSpec((None, CHUNK, WIDTH), lambda b, c: (b, c, 0)),
                   pl.BlockSpec((None, HEADS, HEAD_DIM, HEAD_DIM), lambda b, c: (b, 0, 0, 0))],
        out_shape=[jax.ShapeDtypeStruct((batch, length, WIDTH), jnp.float32),
                   jax.ShapeDtypeStruct((batch, HEADS, HEAD_DIM, HEAD_DIM), jnp.float32)],
        scratch_shapes=[pltpu.VMEM((HEADS, HEAD_DIM, HEAD_DIM), jnp.float32),
                        pltpu.VMEM((CHUNK, WIDTH), jnp.float32),
                        pltpu.VMEM((CHUNK, WIDTH), jnp.float32)],
        compiler_params=pltpu.CompilerParams(
            dimension_semantics=("parallel", "arbitrary"), vmem_limit_bytes=VMEM_LIMIT),
        name="hgrn2",
    )(z4, z4, z4, z4, s0, lb, gain, tri)
    return o.reshape(batch * length, WIDTH), s_fin


def _attn_init(q_ref, q1_ref, q2_ref, m_ref, l_ref, acc_ref):
    lane = lax.broadcasted_iota(jnp.int32, q_ref.shape, 1)
    q = q_ref[...] * (B_DH ** -0.5)
    q1_ref[...] = _bf(jnp.where(lane < B_DH, q, 0.0))
    q2_ref[...] = _bf(jnp.where(lane >= B_DH, q, 0.0))
    m_ref[...] = jnp.full(m_ref.shape, -jnp.inf, jnp.float32)
    l_ref[...] = jnp.zeros(l_ref.shape, jnp.float32)
    acc_ref[...] = jnp.zeros(acc_ref.shape, jnp.float32)


def _attn_update(bias, kb, vb, q1_ref, q2_ref, m_ref, l_ref, acc_ref):
    for m, qm_ref in enumerate((q1_ref, q2_ref)):
        s = lax.dot_general(qm_ref[...], kb, _NT, preferred_element_type=jnp.float32) + bias
        m_prev = m_ref[m]
        m_new = jnp.maximum(m_prev, jnp.max(s, axis=1, keepdims=True))
        alpha = jnp.exp(m_prev - m_new)
        p = jnp.exp(s - m_new)
        l_ref[m] = alpha * l_ref[m] + jnp.sum(p, axis=1, keepdims=True)
        acc_ref[m] = alpha * acc_ref[m] + jnp.dot(_bf(p), vb, preferred_element_type=jnp.float32)
        m_ref[m] = m_new


def _attn_finish(lam, lam_init, g_ref, o_ref, l_ref, acc_ref):
    o = acc_ref[0] / l_ref[0] - lam * (acc_ref[1] / l_ref[1])
    ms = jnp.mean(o * o, axis=1, keepdims=True)
    o_ref[...] = o * lax.rsqrt(ms + RMS_EPS) * g_ref[...] * (1.0 - lam_init)


def _attn_prompt_kernel(qt_ref, kt_ref, par_ref, q_ref, k_ref, v_ref, pat_ref, g_ref, o_ref,
                        q1_ref, q2_ref, m_ref, l_ref, acc_ref, *, t, lam_init):
    h = pl.program_id(1)
    pair = pl.program_id(2)
    qi = qt_ref[pair]
    ki = kt_ref[pair]
    slope = par_ref[1 + h]

    @pl.when(ki == 0)
    def _():
        _attn_init(q_ref, q1_ref, q2_ref, m_ref, l_ref, acc_ref)

    kb = _bf(k_ref[...])
    vb = _bf(v_ref[...])

    @pl.when(ki < qi)
    def _():
        col = lax.broadcasted_iota(jnp.int32, (1, t), 1)
        bias = slope * ((ki - qi) * t + col).astype(jnp.float32)
        _attn_update(bias, kb, vb, q1_ref, q2_ref, m_ref, l_ref, acc_ref)

    @pl.when(ki == qi)
    def _():
        _attn_update(slope * pat_ref[...], kb, vb, q1_ref, q2_ref, m_ref, l_ref, acc_ref)
        _attn_finish(par_ref[0], lam_init, g_ref, o_ref, l_ref, acc_ref)


def _attn_cached_kernel(par_ref, q_ref, k_ref, v_ref, g_ref, o_ref,
                        q1_ref, q2_ref, m_ref, l_ref, acc_ref, *, qpos0, s_valid, lam_init):
    tq, tk = q_ref.shape[0], k_ref.shape[0]
    slope = par_ref[1 + pl.program_id(1)]
    _attn_init(q_ref, q1_ref, q2_ref, m_ref, l_ref, acc_ref)
    qpos = qpos0 + lax.broadcasted_iota(jnp.int32, (tq, tk), 0)
    kpos = lax.broadcasted_iota(jnp.int32, (tq, tk), 1)
    visible = ((kpos // CHUNK) <= (qpos // CHUNK)) & (kpos < s_valid)
    bias = jnp.where(visible, -slope * jnp.abs(qpos - kpos).astype(jnp.float32), -jnp.inf)
    _attn_update(bias, _bf(k_ref[...]), _bf(v_ref[...]), q1_ref, q2_ref, m_ref, l_ref, acc_ref)
    _attn_finish(par_ref[0], lam_init, g_ref, o_ref, l_ref, acc_ref)


def _attn_scratch(tq):
    return [pltpu.VMEM((tq, HEAD_DIM), jnp.bfloat16),
            pltpu.VMEM((tq, HEAD_DIM), jnp.bfloat16),
            pltpu.VMEM((2, tq, 1), jnp.float32),
            pltpu.VMEM((2, tq, 1), jnp.float32),
            pltpu.VMEM((2, tq, HEAD_DIM), jnp.float32)]


ATTN_BLOCK = 1024


Q_GROUP = 4


def _diff_attn_prompt(params, z4, k, v, gain, *, lam_init):
    _, batch, length, _ = z4.shape
    t = ATTN_BLOCK
    assert length % t == 0 and t % CHUNK == 0
    nb = length // t
    pairs = [(qi, ki) for qi in range(nb) for ki in range(qi + 1)]
    qt = jnp.array([p[0] for p in pairs], jnp.int32)
    kt = jnp.array([p[1] for p in pairs], jnp.int32)
    r = lax.broadcasted_iota(jnp.int32, (t, t), 0)
    c = lax.broadcasted_iota(jnp.int32, (t, t), 1)
    pattern = jnp.where(c // CHUNK <= r // CHUNK, jnp.minimum(c, 2 * r - c).astype(jnp.float32), -jnp.inf)

    q_map = lambda b, h, p, qt, kt, par: (b, qt[p], h)
    k_map = lambda b, h, p, qt, kt, par: (b, kt[p], h)
    return pl.pallas_call(
        functools.partial(_attn_prompt_kernel, t=t, lam_init=lam_init),
        grid_spec=pltpu.PrefetchScalarGridSpec(
            num_scalar_prefetch=3,
            grid=(batch, HEADS, len(pairs)),
            in_specs=[pl.BlockSpec((None, None, t, HEAD_DIM), lambda b, h, p, qt, kt, par: (Q_GROUP, b, qt[p], h)),
                      pl.BlockSpec((None, t, HEAD_DIM), k_map),
                      pl.BlockSpec((None, t, HEAD_DIM), k_map),
                      pl.BlockSpec((t, t), lambda b, h, p, qt, kt, par: (0, 0)),
                      pl.BlockSpec((1, HEAD_DIM), lambda b, h, p, qt, kt, par: (0, 0))],
            out_specs=pl.BlockSpec((None, t, HEAD_DIM), q_map),
            scratch_shapes=_attn_scratch(t)),
        out_shape=jax.ShapeDtypeStruct((batch, length, WIDTH), jnp.float32),
        compiler_params=pltpu.CompilerParams(
            dimension_semantics=("parallel", "parallel", "arbitrary"), vmem_limit_bytes=VMEM_LIMIT),
        name="diff_attn_prompt",
    )(qt, kt, params, z4, k, v, pattern, gain)


def _diff_attn_cached(params, z4, k, v, gain, *, qpos0, s_valid, lam_init):
    _, batch, t, _ = z4.shape
    s = k.shape[1]
    return pl.pallas_call(
        functools.partial(_attn_cached_kernel, qpos0=qpos0, s_valid=s_valid, lam_init=lam_init),
        grid_spec=pltpu.PrefetchScalarGridSpec(
            num_scalar_prefetch=1,
            grid=(batch, HEADS),
            in_specs=[pl.BlockSpec((None, None, t, HEAD_DIM), lambda b, h, par: (Q_GROUP, b, 0, h)),
                      pl.BlockSpec((None, s, HEAD_DIM), lambda b, h, par: (b, 0, h)),
                      pl.BlockSpec((None, s, HEAD_DIM), lambda b, h, par: (b, 0, h)),
                      pl.BlockSpec((1, HEAD_DIM), lambda b, h, par: (0, 0))],
            out_specs=pl.BlockSpec((None, t, HEAD_DIM), lambda b, h, par: (b, 0, h)),
            scratch_shapes=_attn_scratch(t)),
        out_shape=jax.ShapeDtypeStruct((batch, t, WIDTH), jnp.float32),
        compiler_params=pltpu.CompilerParams(
            dimension_semantics=("parallel", "parallel"), vmem_limit_bytes=VMEM_LIMIT),
        name="diff_attn_cached",
    )(params, z4, k, v, gain)


def _layer_norm(y, g, b):
    mu = jnp.mean(y, axis=1, keepdims=True)
    d = y - mu
    var = jnp.mean(d * d, axis=1, keepdims=True)
    return d * lax.rsqrt(var + LN_EPS) * g + b


def _out_proj_kernel(oa_ref, ob_ref, x_ref, w_ref, g_ref, b_ref, o_ref):
    acc = jnp.dot(_bf(oa_ref[...]), w_ref[0:WIDTH, :], preferred_element_type=jnp.float32)
    acc += jnp.dot(_bf(ob_ref[...]), w_ref[WIDTH:2 * WIDTH, :], preferred_element_type=jnp.float32)
    o_ref[...] = _layer_norm(DEEPNORM_ALPHA * x_ref[...] + acc, g_ref[...], b_ref[...])


def _out_proj(oa, ob, x2d, w_bf16, g, b):
    n = x2d.shape[0]
    tm = 256
    assert n % tm == 0
    row = lambda i: (i, 0)
    const = lambda i: (0, 0)
    return pl.pallas_call(
        _out_proj_kernel,
        grid=(n // tm,),
        in_specs=[pl.BlockSpec((tm, WIDTH), row), pl.BlockSpec((tm, WIDTH), row),
                  pl.BlockSpec((tm, D_MODEL), row), pl.BlockSpec((2 * WIDTH, D_MODEL), const),
                  pl.BlockSpec((1, D_MODEL), const), pl.BlockSpec((1, D_MODEL), const)],
        out_specs=pl.BlockSpec((tm, D_MODEL), row),
        out_shape=jax.ShapeDtypeStruct((n, D_MODEL), jnp.float32),
        compiler_params=pltpu.CompilerParams(dimension_semantics=("parallel",), vmem_limit_bytes=VMEM_LIMIT),
        name="out_proj",
    )(oa, ob, x2d, w_bf16, g, b)


def _top16(vals, pos, ids=None):
    top_v, top_i = [], []
    for _ in range(PEER_TOPK):
        m = jnp.max(vals, axis=0, keepdims=True)
        first = jnp.min(jnp.where(vals == m, pos, jnp.inf), axis=0, keepdims=True)
        hit = pos == first
        top_v.append(m)
        top_i.append(first if ids is None else jnp.max(jnp.where(hit, ids, -1.0), axis=0, keepdims=True))
        vals = jnp.where(hit, -jnp.inf, vals)
    return jnp.concatenate(top_v, axis=0), jnp.concatenate(top_i, axis=0)


def _pair_candidates(sv, si):
    t = sv[0].shape[1]
    row8 = lax.broadcasted_iota(jnp.int32, (ROWS, t), 0).astype(jnp.float32)
    row16 = lax.broadcasted_iota(jnp.int32, (PEER_TOPK, t), 0).astype(jnp.float32)
    vals = [sv[0][0:1] + sv[1]]
    pos = [row16]
    ids = [si[0][0:1] * N_KEYS + si[1]]
    for a in range(1, ROWS):
        n_b = PEER_TOPK // (a + 1)
        vals.append(jnp.where(row8 < n_b, sv[0][a:a + 1] + sv[1][0:ROWS], -jnp.inf))
        pos.append(row8 + float(a * PEER_TOPK))
        ids.append(si[0][a:a + 1] * N_KEYS + si[1][0:ROWS])
    vals.append(sv[0][ROWS:] + sv[1][0:1])
    pos.append((row8 + float(ROWS)) * float(PEER_TOPK))
    ids.append(si[0][ROWS:] * N_KEYS + si[1][0:1])
    return jnp.concatenate(vals, axis=0), jnp.concatenate(pos, axis=0), jnp.concatenate(ids, axis=0)


def _peer_topk_kernel(x_ref, wq_ref, sk_ref, idx_ref, gate_ref):
    q = jnp.dot(_bf(x_ref[...]), wq_ref[...], preferred_element_type=jnp.float32)
    key_id = lax.broadcasted_iota(jnp.int32, (N_KEYS, TOKEN_BLOCK), 0).astype(jnp.float32)
    for h in range(PEER_HEADS):
        sv, si = [], []
        for c in range(2):
            col = (2 * h + c) * HEAD_DIM
            s_t = _dot_nt(sk_ref[2 * h + c], q[:, col:col + HEAD_DIM])
            v16, i16 = _top16(s_t, key_id)
            sv.append(v16)
            si.append(i16)
        cv, eidx = _top16(*_pair_candidates(sv, si))
        e = jnp.exp(cv - cv[0:1])
        rows = slice(h * PEER_TOPK, (h + 1) * PEER_TOPK)
        gate_ref[0, rows, :] = e / jnp.sum(e, axis=0, keepdims=True)
        idx_ref[0, rows, :] = eidx.astype(jnp.int32)


def _peer_topk(x2d, wq_bf16, sk_bf16):
    n = x2d.shape[0]
    nb = n // TOKEN_BLOCK
    return pl.pallas_call(
        _peer_topk_kernel,
        grid=(nb,),
        in_specs=[pl.BlockSpec((TOKEN_BLOCK, D_MODEL), lambda i: (i, 0)),
                  pl.BlockSpec((D_MODEL, D_MODEL), lambda i: (0, 0)),
                  pl.BlockSpec((2 * PEER_HEADS, N_KEYS, HEAD_DIM), lambda i: (0, 0, 0))],
        out_specs=[pl.BlockSpec((1, PEER_SLOTS, TOKEN_BLOCK), lambda i: (i, 0, 0)),
                   pl.BlockSpec((1, PEER_SLOTS, TOKEN_BLOCK), lambda i: (i, 0, 0))],
        out_shape=[jax.ShapeDtypeStruct((nb, PEER_SLOTS, TOKEN_BLOCK), jnp.int32),
                   jax.ShapeDtypeStruct((nb, PEER_SLOTS, TOKEN_BLOCK), jnp.float32)],
        compiler_params=pltpu.CompilerParams(dimension_semantics=("parallel",), vmem_limit_bytes=VMEM_LIMIT),
        name="peer_topk",
    )(x2d, wq_bf16, sk_bf16)


PEER_TB = 16
PEER_AHEAD = 8


def _pack_tables(u, v):
    ub = lax.bitcast_convert_type(_bf(u), jnp.uint16).astype(jnp.uint32)
    vb = lax.bitcast_convert_type(_bf(v), jnp.uint16).astype(jnp.uint32)
    return ((ub << 16) | vb).reshape(u.shape[0], 1, u.shape[1])


def _peer_mix_kernel(idx_ref, idx_next_ref, gate_ref, x_ref, tab_hbm, g_ref, b_ref, o_ref,
                     buf, xb_ref, y_ref, sem, *, steps):
    i = pl.program_id(0)
    groups = PEER_SLOTS // ROWS
    per_group = PEER_SLOTS // (2 * groups)

    def start_rows(ids_ref, tok, j0, j1):
        for j in range(j0, j1):
            e = ids_ref[0, 0, tok * PEER_SLOTS + j]
            pltpu.make_async_copy(tab_hbm.at[e], buf.at[tok, pl.ds(j, 1)], sem.at[tok]).start(priority=j % 2)

    def wait_token(tok):
        pltpu.make_async_copy(tab_hbm.at[pl.ds(0, PEER_SLOTS), 0], buf.at[tok], sem.at[tok]).wait()

    def ahead(t):
        nxt = t + PEER_AHEAD
        return (idx_ref, nxt) if nxt < PEER_TB else (idx_next_ref, nxt - PEER_TB)

    def u_phase(t, req):
        wait_token(t)
        parts = []
        for g in range(groups):
            start_rows(*req, per_group * g, per_group * (g + 1))
            r0 = ROWS * g
            words = buf[t, r0:r0 + ROWS, :]
            p = lax.bitcast_convert_type(words & jnp.uint32(0xFFFF0000), jnp.float32) * xb_ref[t]
            acc = p[:, 0:LANES]
            for c0 in range(LANES, D_MODEL, LANES):
                acc = acc + p[:, c0:c0 + LANES]
            parts.append(acc)
        act = jnp.sum(jnp.concatenate(parts, axis=0), axis=1, keepdims=True)
        act = 0.5 * act * (1.0 + lax.erf(act * (2.0 ** -0.5)))
        return gate_ref[0, :, t:t + 1] * act

    def v_phase(t, w, req):
        y_acc = jnp.zeros((ROWS, D_MODEL), jnp.float32)
        half = PEER_SLOTS // 2
        for g in range(groups):
            if req is not None:
                start_rows(*req, half + per_group * g, half + per_group * (g + 1))
            r0 = ROWS * g
            words = buf[t, r0:r0 + ROWS, :]
            y_acc = y_acc + lax.bitcast_convert_type(words << 16, jnp.float32) * w[r0:r0 + ROWS]
        y_ref[t:t + 1, :] = jnp.sum(y_acc, axis=0, keepdims=True)

    @pl.when(i == 0)
    def _():
        for t in range(PEER_AHEAD):
            start_rows(idx_ref, t, 0, PEER_SLOTS)

    for t in range(PEER_TB):
        xb_ref[t] = jnp.broadcast_to(x_ref[t:t + 1, :], (ROWS, D_MODEL))

    w_prev = None
    for t in range(PEER_TB):
        w = u_phase(t, ahead(t))
        if t == 0:
            start_rows(*ahead(0), PEER_SLOTS // 2, PEER_SLOTS)
        else:
            v_phase(t - 1, w_prev, ahead(t))
        w_prev = w
    v_phase(PEER_TB - 1, w_prev, None)

    o_ref[...] = _layer_norm(DEEPNORM_ALPHA * x_ref[...] + y_ref[...], g_ref[...], b_ref[...])

    @pl.when(i == steps - 1)
    def _():
        for t in range(PEER_AHEAD):
            wait_token(t)


def _peer_mix(idx, gate, x2d, table, g, b):
    n = x2d.shape[0]
    steps = n // PEER_TB
    assert n % PEER_TB == 0 and PEER_AHEAD < PEER_TB
    idx_tm = jnp.swapaxes(idx, 1, 2).reshape(steps, 1, PEER_TB * PEER_SLOTS)
    gate_tm = jnp.swapaxes(jnp.swapaxes(gate, 1, 2).reshape(steps, PEER_TB, PEER_SLOTS), 1, 2)
    ids_block = (1, 1, PEER_TB * PEER_SLOTS)
    return pl.pallas_call(
        functools.partial(_peer_mix_kernel, steps=steps),
        grid=(steps,),
        in_specs=[pl.BlockSpec(ids_block, lambda i: (i, 0, 0), memory_space=pltpu.SMEM),
                  pl.BlockSpec(ids_block, lambda i: (jnp.minimum(i + 1, steps - 1), 0, 0), memory_space=pltpu.SMEM),
                  pl.BlockSpec((1, PEER_SLOTS, PEER_TB), lambda i: (i, 0, 0)),
                  pl.BlockSpec((PEER_TB, D_MODEL), lambda i: (i, 0)),
                  pl.BlockSpec(memory_space=pl.ANY),
                  pl.BlockSpec((1, D_MODEL), lambda i: (0, 0)),
                  pl.BlockSpec((1, D_MODEL), lambda i: (0, 0))],
        out_specs=pl.BlockSpec((PEER_TB, D_MODEL), lambda i: (i, 0)),
        out_shape=jax.ShapeDtypeStruct((n, D_MODEL), jnp.float32),
        scratch_shapes=[pltpu.VMEM((PEER_TB, PEER_SLOTS, D_MODEL), jnp.uint32),
                        pltpu.VMEM((PEER_TB, ROWS, D_MODEL), jnp.float32),
                        pltpu.VMEM((PEER_TB, D_MODEL), jnp.float32),
                        pltpu.SemaphoreType.DMA((PEER_TB,))],
        compiler_params=pltpu.CompilerParams(dimension_semantics=("arbitrary",), vmem_limit_bytes=VMEM_LIMIT),
        name="peer_mix",
    )(idx_tm, idx_tm, gate_tm, x2d, table, g, b)


def _trunk_layer(x, past_k, past_v, s0, lb, attn_params, lam_init, w_in, a_gain, b_gain, w_out,
                 ln1_g, ln1_b, wq, sub_keys, table, ln2_g, ln2_b):
    batch, length, _ = x.shape
    n = batch * length
    x2d = x.reshape(n, D_MODEL)
    z, k2d, v2d = _in_proj(x2d, w_in)
    z4 = z.reshape(N_MIX_GROUPS, batch, length, WIDTH)
    k3d = k2d.reshape(batch, length, WIDTH)
    v3d = v2d.reshape(batch, length, WIDTH)

    if s0 is None:
        s0 = jnp.zeros((batch, HEADS, HEAD_DIM, HEAD_DIM), jnp.float32)
    o_a, s_new = _hgrn(z4, s0, lb, a_gain)

    if past_k is None:
        o_b = _diff_attn_prompt(attn_params, z4, k3d, v3d, b_gain, lam_init=lam_init)
    else:
        past = past_k.shape[1]
        total = past + length
        padded = -(-total // HEAD_DIM) * HEAD_DIM
        pad = jnp.zeros((batch, padded - total, WIDTH), jnp.float32)
        k_all = jnp.concatenate([past_k.reshape(batch, past, WIDTH), k3d, pad], axis=1)
        v_all = jnp.concatenate([past_v.reshape(batch, past, WIDTH), v3d, pad], axis=1)
        o_b = _diff_attn_cached(attn_params, z4, k_all, v_all, b_gain, qpos0=past, s_valid=total, lam_init=lam_init)
    k_new = k2d.reshape(batch, length, HEADS, HEAD_DIM)
    v_new = v2d.reshape(batch, length, HEADS, HEAD_DIM)

    x1 = _out_proj(o_a, o_b.reshape(n, WIDTH), x2d, w_out, ln1_g, ln1_b)
    idx, gate = _peer_topk(x1, wq, sub_keys)
    y = _peer_mix(idx, gate, x1, table, ln2_g, ln2_b)
    return y.reshape(batch, length, D_MODEL), k_new, v_new, s_new


def kernel(x_prompt, x_sample, cache_k, cache_v, state_hgrn, w_in, hgrn_lb, hgrn_norm_g, diff_lq1, diff_lk1,
           diff_lq2, diff_lk2, diff_norm_g, w_out, ln1_g, ln1_b, peer_wq, peer_sub_keys, peer_u, peer_v,
           ln2_g, ln2_b):
    f32 = jnp.float32
    lower_bounds = jnp.cumsum(jax.nn.softmax(hgrn_lb.astype(f32), axis=0), axis=0)
    slopes = 2.0 ** (-8.0 * jnp.arange(1, HEADS + 1, dtype=f32) / HEADS)
    y_p, y_s = x_prompt, x_sample
    outs = [[] for _ in range(6)]
    for l in range(DEPTH):
        lam_init = 0.8 - 0.6 * math.exp(-0.3 * l)
        lam = (jnp.exp(jnp.sum(diff_lq1[l].astype(f32) * diff_lk1[l].astype(f32)))
               - jnp.exp(jnp.sum(diff_lq2[l].astype(f32) * diff_lk2[l].astype(f32))) + lam_init)
        attn_params = jnp.concatenate([lam.reshape(1), slopes])
        shared = (lower_bounds[l].reshape(1, WIDTH), attn_params, lam_init, _bf(w_in[l]),
                  hgrn_norm_g[l].reshape(1, HEAD_DIM), diff_norm_g[l].reshape(1, HEAD_DIM), _bf(w_out[l]),
                  ln1_g[l].reshape(1, D_MODEL), ln1_b[l].reshape(1, D_MODEL), _bf(peer_wq[l]),
                  _bf(peer_sub_keys[l].reshape(2 * PEER_HEADS, N_KEYS, HEAD_DIM)), _pack_tables(peer_u[l], peer_v[l]),
                  ln2_g[l].reshape(1, D_MODEL), ln2_b[l].reshape(1, D_MODEL))
        y_p, kp, vp, sp = _trunk_layer(y_p, None, None, None, *shared)
        y_s, kn, vn, sn = _trunk_layer(y_s, cache_k[l], cache_v[l], state_hgrn[l], *shared)
        for lst, val in zip(outs, (kp, vp, sp, kn, vn, sn)):
            lst.append(val)
    return (y_p, y_s) + tuple(jnp.stack(o) for o in outs)
```

```python
import functools
import math

import jax
import jax.numpy as jnp
from jax import lax
from jax.experimental import pallas as pl
from jax.experimental.pallas import tpu as pltpu

D_MODEL = 2048
CHUNK = 64
HEADS = 8
HEAD_DIM = 128
WIDTH = HEADS * HEAD_DIM
N_GROUPS = 7
B_DH = 64
PEER_HEADS = 8
N_KEYS = 128
PEER_TOPK = 16
PEER_SLOTS = PEER_HEADS * PEER_TOPK
TOKEN_BLOCK = 128
LN_EPS = 1e-5
RMS_EPS = 1e-5
DEPTH = 1
DEEPNORM_ALPHA = (2.0 * DEPTH) ** 0.25
ROWS, LANES = 8, 128
SUB = 16
LANE_GROUP = 512
VMEM_LIMIT = 48 * 1024 * 1024

_NT = (((1,), (1,)), ((), ()))
_TN = (((0,), (0,)), ((), ()))


def _bf(x):
    return x.astype(jnp.bfloat16)


def _dot(a, b):
    return jnp.dot(_bf(a), _bf(b), preferred_element_type=jnp.float32)


def _dot_nt(a, b):
    return lax.dot_general(_bf(a), _bf(b), _NT, preferred_element_type=jnp.float32)


def _dot_tn(a, b):
    return lax.dot_general(_bf(a), _bf(b), _TN, preferred_element_type=jnp.float32)


N_MIX_GROUPS = 5


def _in_proj_kernel(x_ref, w_ref, z_ref, k_ref, v_ref):
    j = pl.program_id(1)
    z = jnp.dot(_bf(x_ref[...]), w_ref[...], preferred_element_type=jnp.float32)

    @pl.when(j < N_MIX_GROUPS)
    def _():
        z_ref[...] = z

    @pl.when(j == N_MIX_GROUPS)
    def _():
        k_ref[...] = z

    @pl.when(j == N_MIX_GROUPS + 1)
    def _():
        v_ref[...] = z


def _in_proj(x2d, w_bf16):
    n = x2d.shape[0]
    tm = 512
    assert n % tm == 0
    return pl.pallas_call(
        _in_proj_kernel,
        grid=(n // tm, N_GROUPS),
        in_specs=[pl.BlockSpec((tm, D_MODEL), lambda i, j: (i, 0)),
                  pl.BlockSpec((D_MODEL, WIDTH), lambda i, j: (0, j))],
        out_specs=[pl.BlockSpec((None, tm, WIDTH), lambda i, j: (jnp.minimum(j, N_MIX_GROUPS - 1), i, 0)),
                   pl.BlockSpec((tm, WIDTH), lambda i, j: (i, 0)),
                   pl.BlockSpec((tm, WIDTH), lambda i, j: (i, 0))],
        out_shape=[jax.ShapeDtypeStruct((N_MIX_GROUPS, n, WIDTH), jnp.float32),
                   jax.ShapeDtypeStruct((n, WIDTH), jnp.float32),
                   jax.ShapeDtypeStruct((n, WIDTH), jnp.float32)],
        compiler_params=pltpu.CompilerParams(
            dimension_semantics=("parallel", "arbitrary"), vmem_limit_bytes=VMEM_LIMIT),
        name="in_proj",
    )(x2d, w_bf16)


def _hgrn_kernel(q_ref, fa_ref, v_ref, ga_ref, s0_ref, lb_ref, g_ref, tri_ref,
                 o_ref, sfin_ref, st_ref, b_ref, kk_ref, *, n_chunks):
    c = pl.program_id(1)

    @pl.when(c == 0)
    def _():
        for h in range(HEADS):
            st_ref[h] = s0_ref[h].T

    lb = lb_ref[...]
    f = lb + (1.0 - lb) * jax.nn.sigmoid(fa_ref[...])
    lf = jnp.log(f)
    kk_ref[...] = 1.0 - f
    lf_hi = _bf(lf)
    lf_lo = _bf(lf - lf_hi.astype(jnp.float32))
    tri = tri_ref[...]
    b_ref[...] = (jnp.dot(tri, lf_hi, preferred_element_type=jnp.float32)
                  + jnp.dot(tri, lf_lo, preferred_element_type=jnp.float32))

    for h in range(HEADS):
        hs = slice(h * HEAD_DIM, (h + 1) * HEAD_DIM)
        bh = b_ref[:, hs]
        qh = q_ref[:, hs]
        kh = kk_ref[:, hs]
        vh = v_ref[:, hs]
        st = st_ref[h]
        o_h = _dot_nt(qh * jnp.exp(bh), st)
        parts = [o_h[0:SUB]]
        for i in range(1, CHUNK // SUB):
            r0 = i * SUB
            b_start = bh[r0 - 1:r0]
            q_i = qh[r0:r0 + SUB] * jnp.exp(bh[r0:r0 + SUB] - b_start)
            k_p = kh[0:r0] * jnp.exp(b_start - bh[0:r0])
            sc = _dot_nt(q_i, k_p)
            parts.append(o_h[r0:r0 + SUB] + _dot(sc, vh[0:r0]))
        o_ref[:, hs] = jnp.concatenate(parts, axis=0)
        b_end = bh[CHUNK - 1:CHUNK]
        kd = kh * jnp.exp(b_end - bh)
        st_ref[h] = st * jnp.exp(b_end) + _dot_tn(vh, kd)

    row = lax.broadcasted_iota(jnp.int32, (SUB, LANE_GROUP), 0)
    for i in range(CHUNK // SUB):
        r0 = i * SUB
        for g0 in range(0, WIDTH, LANE_GROUP):
            ls = slice(g0, g0 + LANE_GROUP)
            b_i = b_ref[r0:r0 + SUB, ls]
            q_i = q_ref[r0:r0 + SUB, ls]

            def body(s, acc, r0=r0, ls=ls, b_i=b_i, q_i=q_i):
                b_s = b_ref[pl.ds(r0 + s, 1), ls]
                k_s = kk_ref[pl.ds(r0 + s, 1), ls]
                v_s = v_ref[pl.ds(r0 + s, 1), ls]
                e = jnp.exp(jnp.where(row >= s, b_i - b_s, -jnp.inf))
                p = q_i * (k_s * e)
                cols = []
                for h in range(LANE_GROUP // HEAD_DIM):
                    hs = slice(h * HEAD_DIM, (h + 1) * HEAD_DIM)
                    w = jnp.sum(p[:, hs], axis=1, keepdims=True)
                    cols.append(w * v_s[:, hs])
                return acc + jnp.concatenate(cols, axis=1)

            acc = lax.fori_loop(0, SUB, body, jnp.zeros((SUB, LANE_GROUP), jnp.float32), unroll=True)
            o_ref[r0:r0 + SUB, ls] += acc

    gate = jax.nn.sigmoid(ga_ref[...])
    gain = g_ref[...]
    for h in range(HEADS):
        hs = slice(h * HEAD_DIM, (h + 1) * HEAD_DIM)
        o_h = o_ref[:, hs]
        ms = jnp.mean(o_h * o_h, axis=1, keepdims=True)
        o_ref[:, hs] = o_h * lax.rsqrt(ms + RMS_EPS) * gain * gate[:, hs]

    @pl.when(c == n_chunks - 1)
    def _():
        for h in range(HEADS):
            sfin_ref[h] = st_ref[h].T


def _hgrn(z4, s0, lb, gain):
    _, batch, length, _ = z4.shape
    n_chunks = length // CHUNK
    tri = jnp.tril(jnp.ones((CHUNK, CHUNK), jnp.float32)).astype(jnp.bfloat16)

    def zspec(group):
        return pl.BlockSpec((None, None, CHUNK, WIDTH), lambda b, c, group=group: (group, b, c, 0))

    o, s_fin = pl.pallas_call(
        functools.partial(_hgrn_kernel, n_chunks=n_chunks),
        grid=(batch, n_chunks),
        in_specs=[zspec(0), zspec(1), zspec(2), zspec(3),
                  pl.BlockSpec((None, HEADS, HEAD_DIM, HEAD_DIM), lambda b, c: (b, 0, 0, 0)),
                  pl.BlockSpec((1, WIDTH), lambda b, c: (0, 0)),
                  pl.BlockSpec((1, HEAD_DIM), lambda b, c: (0, 0)),
                  pl.BlockSpec((CHUNK, CHUNK), lambda b, c: (0, 0))],
        out_specs=[pl.BlockSpec((None, CHUNK, WIDTH), lambda b, c: (b, c, 0)),
                   pl.BlockSpec((None, HEADS, HEAD_DIM, HEAD_DIM), lambda b, c: (b, 0, 0, 0))],
        out_shape=[jax.ShapeDtypeStruct((batch, length, WIDTH), jnp.float32),
                   jax.ShapeDtypeStruct((batch, HEADS, HEAD_DIM, HEAD_DIM), jnp.float32)],
        scratch_shapes=[pltpu.VMEM((HEADS, HEAD_DIM, HEAD_DIM), jnp.float32),
                        pltpu.VMEM((CHUNK, WIDTH), jnp.float32),
                        pltpu.VMEM((CHUNK, WIDTH), jnp.float32)],
        compiler_params=pltpu.CompilerParams(
            dimension_semantics=("parallel", "arbitrary"), vmem_limit_bytes=VMEM_LIMIT),
        name="hgrn2",
    )(z4, z4, z4, z4, s0, lb, gain, tri)
    return o.reshape(batch * length, WIDTH), s_fin


def _attn_init(q_ref, q1_ref, q2_ref, m_ref, l_ref, acc_ref):
    lane = lax.broadcasted_iota(jnp.int32, q_ref.shape, 1)
    q = q_ref[...] * (B_DH ** -0.5)
    q1_ref[...] = _bf(jnp.where(lane < B_DH, q, 0.0))
    q2_ref[...] = _bf(jnp.where(lane >= B_DH, q, 0.0))
    m_ref[...] = jnp.full(m_ref.shape, -jnp.inf, jnp.float32)
    l_ref[...] = jnp.zeros(l_ref.shape, jnp.float32)
    acc_ref[...] = jnp.zeros(acc_ref.shape, jnp.float32)


def _attn_update(bias, kb, vb, q1_ref, q2_ref, m_ref, l_ref, acc_ref):
    for m, qm_ref in enumerate((q1_ref, q2_ref)):
        s = lax.dot_general(qm_ref[...], kb, _NT, preferred_element_type=jnp.float32) + bias
        m_prev = m_ref[m]
        m_new = jnp.maximum(m_prev, jnp.max(s, axis=1, keepdims=True))
        alpha = jnp.exp(m_prev - m_new)
        p = jnp.exp(s - m_new)
        l_ref[m] = alpha * l_ref[m] + jnp.sum(p, axis=1, keepdims=True)
        acc_ref[m] = alpha * acc_ref[m] + jnp.dot(_bf(p), vb, preferred_element_type=jnp.float32)
        m_ref[m] = m_new


def _attn_finish(lam, lam_init, g_ref, o_ref, l_ref, acc_ref):
    o = acc_ref[0] / l_ref[0] - lam * (acc_ref[1] / l_ref[1])
    ms = jnp.mean(o * o, axis=1, keepdims=True)
    o_ref[...] = o * lax.rsqrt(ms + RMS_EPS) * g_ref[...] * (1.0 - lam_init)


def _attn_update_t(bias_t, kb, vb, q1_ref, q2_ref, m_ref, l_ref, acc_ref):
    for m, qm_ref in enumerate((q1_ref, q2_ref)):
        s = lax.dot_general(kb, qm_ref[...], _NT, preferred_element_type=jnp.float32) + bias_t
        m_prev = m_ref[m]
        m_new = jnp.maximum(m_prev, jnp.max(s, axis=0, keepdims=True))
        alpha = jnp.exp(m_prev - m_new)
        p = jnp.exp(s - m_new)
        l_ref[m] = alpha * l_ref[m] + jnp.sum(p, axis=0, keepdims=True)
        acc_ref[m] = alpha * acc_ref[m] + lax.dot_general(vb, _bf(p), _TN, preferred_element_type=jnp.float32)
        m_ref[m] = m_new


def _attn_prompt_kernel(qt_ref, kt_ref, par_ref, q_ref, k_ref, v_ref, pat_ref, g_ref, o_ref,
                        q1_ref, q2_ref, m_ref, l_ref, acc_ref, *, t, lam_init):
    h = pl.program_id(1)
    pair = pl.program_id(2)
    qi = qt_ref[pair]
    ki = kt_ref[pair]
    slope = par_ref[1 + h]

    @pl.when(ki == 0)
    def _():
        _attn_init(q_ref, q1_ref, q2_ref, m_ref, l_ref, acc_ref)

    kb = _bf(k_ref[...])
    vb = _bf(v_ref[...])

    @pl.when(ki < qi)
    def _():
        row = lax.broadcasted_iota(jnp.int32, (t, 1), 0)
        bias_t = slope * ((ki - qi) * t + row).astype(jnp.float32)
        _attn_update_t(bias_t, kb, vb, q1_ref, q2_ref, m_ref, l_ref, acc_ref)

    @pl.when(ki == qi)
    def _():
        _attn_update_t(slope * pat_ref[...], kb, vb, q1_ref, q2_ref, m_ref, l_ref, acc_ref)
        o_t = acc_ref[0] / l_ref[0] - par_ref[0] * (acc_ref[1] / l_ref[1])
        ms = jnp.mean(o_t * o_t, axis=0, keepdims=True)
        o_t = o_t * lax.rsqrt(ms + RMS_EPS) * g_ref[...] * (1.0 - lam_init)
        o_ref[...] = o_t.T


def _attn_cached_kernel(par_ref, q_ref, k_ref, v_ref, g_ref, o_ref,
                        q1_ref, q2_ref, m_ref, l_ref, acc_ref, *, qpos0, s_valid, lam_init):
    tq, tk = q_ref.shape[0], k_ref.shape[0]
    slope = par_ref[1 + pl.program_id(1)]
    _attn_init(q_ref, q1_ref, q2_ref, m_ref, l_ref, acc_ref)
    qpos = qpos0 + lax.broadcasted_iota(jnp.int32, (tq, tk), 0)
    kpos = lax.broadcasted_iota(jnp.int32, (tq, tk), 1)
    visible = ((kpos // CHUNK) <= (qpos // CHUNK)) & (kpos < s_valid)
    bias = jnp.where(visible, -slope * jnp.abs(qpos - kpos).astype(jnp.float32), -jnp.inf)
    _attn_update(bias, _bf(k_ref[...]), _bf(v_ref[...]), q1_ref, q2_ref, m_ref, l_ref, acc_ref)
    _attn_finish(par_ref[0], lam_init, g_ref, o_ref, l_ref, acc_ref)


def _attn_scratch(tq):
    return [pltpu.VMEM((tq, HEAD_DIM), jnp.bfloat16),
            pltpu.VMEM((tq, HEAD_DIM), jnp.bfloat16),
            pltpu.VMEM((2, tq, 1), jnp.float32),
            pltpu.VMEM((2, tq, 1), jnp.float32),
            pltpu.VMEM((2, tq, HEAD_DIM), jnp.float32)]


ATTN_BLOCK = 1024


Q_GROUP = 4


def _diff_attn_prompt(params, z4, k, v, gain, *, lam_init):
    _, batch, length, _ = z4.shape
    t = ATTN_BLOCK
    assert length % t == 0 and t % CHUNK == 0
    nb = length // t
    pairs = [(qi, ki) for qi in range(nb) for ki in range(qi + 1)]
    qt = jnp.array([p[0] for p in pairs], jnp.int32)
    kt = jnp.array([p[1] for p in pairs], jnp.int32)
    kk = lax.broadcasted_iota(jnp.int32, (t, t), 0)
    qq = lax.broadcasted_iota(jnp.int32, (t, t), 1)
    pattern = jnp.where(kk // CHUNK <= qq // CHUNK, jnp.minimum(kk, 2 * qq - kk).astype(jnp.float32), -jnp.inf)
    scratch = [pltpu.VMEM((t, HEAD_DIM), jnp.bfloat16),
               pltpu.VMEM((t, HEAD_DIM), jnp.bfloat16),
               pltpu.VMEM((2, 1, t), jnp.float32),
               pltpu.VMEM((2, 1, t), jnp.float32),
               pltpu.VMEM((2, HEAD_DIM, t), jnp.float32)]

    q_map = lambda b, h, p, qt, kt, par: (b, qt[p], h)
    k_map = lambda b, h, p, qt, kt, par: (b, kt[p], h)
    return pl.pallas_call(
        functools.partial(_attn_prompt_kernel, t=t, lam_init=lam_init),
        grid_spec=pltpu.PrefetchScalarGridSpec(
            num_scalar_prefetch=3,
            grid=(batch, HEADS, len(pairs)),
            in_specs=[pl.BlockSpec((None, None, t, HEAD_DIM), lambda b, h, p, qt, kt, par: (Q_GROUP, b, qt[p], h)),
                      pl.BlockSpec((None, t, HEAD_DIM), k_map),
                      pl.BlockSpec((None, t, HEAD_DIM), k_map),
                      pl.BlockSpec((t, t), lambda b, h, p, qt, kt, par: (0, 0)),
                      pl.BlockSpec((HEAD_DIM, 1), lambda b, h, p, qt, kt, par: (0, 0))],
            out_specs=pl.BlockSpec((None, t, HEAD_DIM), q_map),
            scratch_shapes=scratch),
        out_shape=jax.ShapeDtypeStruct((batch, length, WIDTH), jnp.float32),
        compiler_params=pltpu.CompilerParams(
            dimension_semantics=("parallel", "parallel", "arbitrary"), vmem_limit_bytes=VMEM_LIMIT),
        name="diff_attn_prompt",
    )(qt, kt, params, z4, k, v, pattern, gain.reshape(HEAD_DIM, 1))


def _diff_attn_cached(params, z4, k, v, gain, *, qpos0, s_valid, lam_init):
    _, batch, t, _ = z4.shape
    s = k.shape[1]
    return pl.pallas_call(
        functools.partial(_attn_cached_kernel, qpos0=qpos0, s_valid=s_valid, lam_init=lam_init),
        grid_spec=pltpu.PrefetchScalarGridSpec(
            num_scalar_prefetch=1,
            grid=(batch, HEADS),
            in_specs=[pl.BlockSpec((None, None, t, HEAD_DIM), lambda b, h, par: (Q_GROUP, b, 0, h)),
                      pl.BlockSpec((None, s, HEAD_DIM), lambda b, h, par: (b, 0, h)),
                      pl.BlockSpec((None, s, HEAD_DIM), lambda b, h, par: (b, 0, h)),
                      pl.BlockSpec((1, HEAD_DIM), lambda b, h, par: (0, 0))],
            out_specs=pl.BlockSpec((None, t, HEAD_DIM), lambda b, h, par: (b, 0, h)),
            scratch_shapes=_attn_scratch(t)),
        out_shape=jax.ShapeDtypeStruct((batch, t, WIDTH), jnp.float32),
        compiler_params=pltpu.CompilerParams(
            dimension_semantics=("parallel", "parallel"), vmem_limit_bytes=VMEM_LIMIT),
        name="diff_attn_cached",
    )(params, z4, k, v, gain)


def _layer_norm(y, g, b):
    mu = jnp.mean(y, axis=1, keepdims=True)
    d = y - mu
    var = jnp.mean(d * d, axis=1, keepdims=True)
    return d * lax.rsqrt(var + LN_EPS) * g + b


def _out_proj_kernel(oa_ref, ob_ref, x_ref, w_ref, g_ref, b_ref, o_ref):
    acc = jnp.dot(_bf(oa_ref[...]), w_ref[0:WIDTH, :], preferred_element_type=jnp.float32)
    acc += jnp.dot(_bf(ob_ref[...]), w_ref[WIDTH:2 * WIDTH, :], preferred_element_type=jnp.float32)
    o_ref[...] = _layer_norm(DEEPNORM_ALPHA * x_ref[...] + acc, g_ref[...], b_ref[...])


def _out_proj(oa, ob, x2d, w_bf16, g, b):
    n = x2d.shape[0]
    tm = 256
    assert n % tm == 0
    row = lambda i: (i, 0)
    const = lambda i: (0, 0)
    return pl.pallas_call(
        _out_proj_kernel,
        grid=(n // tm,),
        in_specs=[pl.BlockSpec((tm, WIDTH), row), pl.BlockSpec((tm, WIDTH), row),
                  pl.BlockSpec((tm, D_MODEL), row), pl.BlockSpec((2 * WIDTH, D_MODEL), const),
                  pl.BlockSpec((1, D_MODEL), const), pl.BlockSpec((1, D_MODEL), const)],
        out_specs=pl.BlockSpec((tm, D_MODEL), row),
        out_shape=jax.ShapeDtypeStruct((n, D_MODEL), jnp.float32),
        compiler_params=pltpu.CompilerParams(dimension_semantics=("parallel",), vmem_limit_bytes=VMEM_LIMIT),
        name="out_proj",
    )(oa, ob, x2d, w_bf16, g, b)


def _top16(vals, pos, ids=None):
    top_v, top_i = [], []
    for _ in range(PEER_TOPK):
        m = jnp.max(vals, axis=0, keepdims=True)
        first = jnp.min(jnp.where(vals == m, pos, jnp.inf), axis=0, keepdims=True)
        hit = pos == first
        top_v.append(m)
        top_i.append(first if ids is None else jnp.max(jnp.where(hit, ids, -1.0), axis=0, keepdims=True))
        vals = jnp.where(hit, -jnp.inf, vals)
    return jnp.concatenate(top_v, axis=0), jnp.concatenate(top_i, axis=0)


def _pair_candidates(sv, si):
    t = sv[0].shape[1]
    row8 = lax.broadcasted_iota(jnp.int32, (ROWS, t), 0).astype(jnp.float32)
    row16 = lax.broadcasted_iota(jnp.int32, (PEER_TOPK, t), 0).astype(jnp.float32)
    vals = [sv[0][0:1] + sv[1]]
    pos = [row16]
    ids = [si[0][0:1] * N_KEYS + si[1]]
    for a in range(1, ROWS):
        n_b = PEER_TOPK // (a + 1)
        vals.append(jnp.where(row8 < n_b, sv[0][a:a + 1] + sv[1][0:ROWS], -jnp.inf))
        pos.append(row8 + float(a * PEER_TOPK))
        ids.append(si[0][a:a + 1] * N_KEYS + si[1][0:ROWS])
    vals.append(sv[0][ROWS:] + sv[1][0:1])
    pos.append((row8 + float(ROWS)) * float(PEER_TOPK))
    ids.append(si[0][ROWS:] * N_KEYS + si[1][0:1])
    return jnp.concatenate(vals, axis=0), jnp.concatenate(pos, axis=0), jnp.concatenate(ids, axis=0)


def _peer_topk_kernel(x_ref, wq_ref, sk_ref, idx_ref, gate_ref):
    q = jnp.dot(_bf(x_ref[...]), wq_ref[...], preferred_element_type=jnp.float32)
    key_id = lax.broadcasted_iota(jnp.int32, (N_KEYS, TOKEN_BLOCK), 0).astype(jnp.float32)
    for h in range(PEER_HEADS):
        sv, si = [], []
        for c in range(2):
            col = (2 * h + c) * HEAD_DIM
            s_t = _dot_nt(sk_ref[2 * h + c], q[:, col:col + HEAD_DIM])
            v16, i16 = _top16(s_t, key_id)
            sv.append(v16)
            si.append(i16)
        cv, eidx = _top16(*_pair_candidates(sv, si))
        e = jnp.exp(cv - cv[0:1])
        rows = slice(h * PEER_TOPK, (h + 1) * PEER_TOPK)
        gate_ref[0, rows, :] = e / jnp.sum(e, axis=0, keepdims=True)
        idx_ref[0, rows, :] = eidx.astype(jnp.int32)


def _peer_topk(x2d, wq_bf16, sk_bf16):
    n = x2d.shape[0]
    nb = n // TOKEN_BLOCK
    return pl.pallas_call(
        _peer_topk_kernel,
        grid=(nb,),
        in_specs=[pl.BlockSpec((TOKEN_BLOCK, D_MODEL), lambda i: (i, 0)),
                  pl.BlockSpec((D_MODEL, D_MODEL), lambda i: (0, 0)),
                  pl.BlockSpec((2 * PEER_HEADS, N_KEYS, HEAD_DIM), lambda i: (0, 0, 0))],
        out_specs=[pl.BlockSpec((1, PEER_SLOTS, TOKEN_BLOCK), lambda i: (i, 0, 0)),
                   pl.BlockSpec((1, PEER_SLOTS, TOKEN_BLOCK), lambda i: (i, 0, 0))],
        out_shape=[jax.ShapeDtypeStruct((nb, PEER_SLOTS, TOKEN_BLOCK), jnp.int32),
                   jax.ShapeDtypeStruct((nb, PEER_SLOTS, TOKEN_BLOCK), jnp.float32)],
        compiler_params=pltpu.CompilerParams(dimension_semantics=("parallel",), vmem_limit_bytes=VMEM_LIMIT),
        name="peer_topk",
    )(x2d, wq_bf16, sk_bf16)


PEER_TB = 16
PEER_AHEAD = 8


def _pack_tables(u, v):
    ub = lax.bitcast_convert_type(_bf(u), jnp.uint16).astype(jnp.uint32)
    vb = lax.bitcast_convert_type(_bf(v), jnp.uint16).astype(jnp.uint32)
    return ((ub << 16) | vb).reshape(u.shape[0], 1, u.shape[1])


def _peer_mix_kernel(idx_ref, idx_next_ref, gate_ref, x_ref, tab_hbm, g_ref, b_ref, o_ref,
                     buf, xb_ref, y_ref, sem, *, steps):
    i = pl.program_id(0)
    groups = PEER_SLOTS // ROWS
    per_group = PEER_SLOTS // (2 * groups)

    def start_rows(ids_ref, tok, j0, j1):
        for j in range(j0, j1):
            e = ids_ref[0, 0, tok * PEER_SLOTS + j]
            pltpu.make_async_copy(tab_hbm.at[e], buf.at[tok, pl.ds(j, 1)], sem.at[tok]).start(priority=j % 2)

    def wait_token(tok):
        pltpu.make_async_copy(tab_hbm.at[pl.ds(0, PEER_SLOTS), 0], buf.at[tok], sem.at[tok]).wait()

    def ahead(t):
        nxt = t + PEER_AHEAD
        return (idx_ref, nxt) if nxt < PEER_TB else (idx_next_ref, nxt - PEER_TB)

    def u_phase(t, req):
        wait_token(t)
        parts = []
        for g in range(groups):
            start_rows(*req, per_group * g, per_group * (g + 1))
            r0 = ROWS * g
            acc = None
            for c0 in range(0, D_MODEL, LANES):
                words = buf[t, r0:r0 + ROWS, c0:c0 + LANES]
                term = (lax.bitcast_convert_type(words & jnp.uint32(0xFFFF0000), jnp.float32)
                        * xb_ref[t, :, c0:c0 + LANES])
                acc = term if acc is None else acc + term
            parts.append(acc)
        act = jnp.sum(jnp.concatenate(parts, axis=0), axis=1, keepdims=True)
        act = 0.5 * act * (1.0 + lax.erf(act * (2.0 ** -0.5)))
        return gate_ref[0, :, t:t + 1] * act

    def v_phase(t, w, req):
        y_acc = jnp.zeros((ROWS, D_MODEL), jnp.float32)
        half = PEER_SLOTS // 2
        for g in range(groups):
            if req is not None:
                start_rows(*req, half + per_group * g, half + per_group * (g + 1))
            r0 = ROWS * g
            words = buf[t, r0:r0 + ROWS, :]
            y_acc = y_acc + lax.bitcast_convert_type(words << 16, jnp.float32) * w[r0:r0 + ROWS]
        y_ref[t:t + 1, :] = jnp.sum(y_acc, axis=0, keepdims=True)

    @pl.when(i == 0)
    def _():
        for t in range(PEER_AHEAD):
            start_rows(idx_ref, t, 0, PEER_SLOTS)

    for t in range(PEER_TB):
        xb_ref[t] = jnp.broadcast_to(x_ref[t:t + 1, :], (ROWS, D_MODEL))

    w_prev = None
    for t in range(PEER_TB):
        w = u_phase(t, ahead(t))
        if t == 0:
            start_rows(*ahead(0), PEER_SLOTS // 2, PEER_SLOTS)
        else:
            v_phase(t - 1, w_prev, ahead(t))
        w_prev = w
    v_phase(PEER_TB - 1, w_prev, None)

    o_ref[...] = _layer_norm(DEEPNORM_ALPHA * x_ref[...] + y_ref[...], g_ref[...], b_ref[...])

    @pl.when(i == steps - 1)
    def _():
        for t in range(PEER_AHEAD):
            wait_token(t)


def _peer_mix(idx, gate, x2d, table, g, b):
    n = x2d.shape[0]
    steps = n // PEER_TB
    assert n % PEER_TB == 0 and PEER_AHEAD < PEER_TB
    idx_tm = jnp.swapaxes(idx, 1, 2).reshape(steps, 1, PEER_TB * PEER_SLOTS)
    gate_tm = jnp.swapaxes(jnp.swapaxes(gate, 1, 2).reshape(steps, PEER_TB, PEER_SLOTS), 1, 2)
    ids_block = (1, 1, PEER_TB * PEER_SLOTS)
    return pl.pallas_call(
        functools.partial(_peer_mix_kernel, steps=steps),
        grid=(steps,),
        in_specs=[pl.BlockSpec(ids_block, lambda i: (i, 0, 0), memory_space=pltpu.SMEM),
                  pl.BlockSpec(ids_block, lambda i: (jnp.minimum(i + 1, steps - 1), 0, 0), memory_space=pltpu.SMEM),
                  pl.BlockSpec((1, PEER_SLOTS, PEER_TB), lambda i: (i, 0, 0)),
                  pl.BlockSpec((PEER_TB, D_MODEL), lambda i: (i, 0)),
                  pl.BlockSpec(memory_space=pl.ANY),
                  pl.BlockSpec((1, D_MODEL), lambda i: (0, 0)),
                  pl.BlockSpec((1, D_MODEL), lambda i: (0, 0))],
        out_specs=pl.BlockSpec((PEER_TB, D_MODEL), lambda i: (i, 0)),
        out_shape=jax.ShapeDtypeStruct((n, D_MODEL), jnp.float32),
        scratch_shapes=[pltpu.VMEM((PEER_TB, PEER_SLOTS, D_MODEL), jnp.uint32),
                        pltpu.VMEM((PEER_TB, ROWS, D_MODEL), jnp.float32),
                        pltpu.VMEM((PEER_TB, D_MODEL), jnp.float32),
                        pltpu.SemaphoreType.DMA((PEER_TB,))],
        compiler_params=pltpu.CompilerParams(dimension_semantics=("arbitrary",), vmem_limit_bytes=VMEM_LIMIT),
        name="peer_mix",
    )(idx_tm, idx_tm, gate_tm, x2d, table, g, b)


def _trunk_layer(x, past_k, past_v, s0, lb, attn_params, lam_init, w_in, a_gain, b_gain, w_out,
                 ln1_g, ln1_b, wq, sub_keys, table, ln2_g, ln2_b):
    batch, length, _ = x.shape
    n = batch * length
    x2d = x.reshape(n, D_MODEL)
    z, k2d, v2d = _in_proj(x2d, w_in)
    z4 = z.reshape(N_MIX_GROUPS, batch, length, WIDTH)
    k3d = k2d.reshape(batch, length, WIDTH)
    v3d = v2d.reshape(batch, length, WIDTH)

    if s0 is None:
        s0 = jnp.zeros((batch, HEADS, HEAD_DIM, HEAD_DIM), jnp.float32)
    o_a, s_new = _hgrn(z4, s0, lb, a_gain)

    if past_k is None:
        o_b = _diff_attn_prompt(attn_params, z4, k3d, v3d, b_gain, lam_init=lam_init)
    else:
        past = past_k.shape[1]
        total = past + length
        padded = -(-total // HEAD_DIM) * HEAD_DIM
        pad = jnp.zeros((batch, padded - total, WIDTH), jnp.float32)
        k_all = jnp.concatenate([past_k.reshape(batch, past, WIDTH), k3d, pad], axis=1)
        v_all = jnp.concatenate([past_v.reshape(batch, past, WIDTH), v3d, pad], axis=1)
        o_b = _diff_attn_cached(attn_params, z4, k_all, v_all, b_gain, qpos0=past, s_valid=total, lam_init=lam_init)
    k_new = k2d.reshape(batch, length, HEADS, HEAD_DIM)
    v_new = v2d.reshape(batch, length, HEADS, HEAD_DIM)

    x1 = _out_proj(o_a, o_b.reshape(n, WIDTH), x2d, w_out, ln1_g, ln1_b)
    idx, gate = _peer_topk(x1, wq, sub_keys)
    y = _peer_mix(idx, gate, x1, table, ln2_g, ln2_b)
    return y.reshape(batch, length, D_MODEL), k_new, v_new, s_new


def kernel(x_prompt, x_sample, cache_k, cache_v, state_hgrn, w_in, hgrn_lb, hgrn_norm_g, diff_lq1, diff_lk1,
           diff_lq2, diff_lk2, diff_norm_g, w_out, ln1_g, ln1_b, peer_wq, peer_sub_keys, peer_u, peer_v,
           ln2_g, ln2_b):
    f32 = jnp.float32
    lower_bounds = jnp.cumsum(jax.nn.softmax(hgrn_lb.astype(f32), axis=0), axis=0)
    slopes = 2.0 ** (-8.0 * jnp.arange(1, HEADS + 1, dtype=f32) / HEADS)
    y_p, y_s = x_prompt, x_sample
    outs = [[] for _ in range(6)]
    for l in range(DEPTH):
        lam_init = 0.8 - 0.6 * math.exp(-0.3 * l)
        lam = (jnp.exp(jnp.sum(diff_lq1[l].astype(f32) * diff_lk1[l].astype(f32)))
               - jnp.exp(jnp.sum(diff_lq2[l].astype(f32) * diff_lk2[l].astype(f32))) + lam_init)
        attn_params = jnp.concatenate([lam.reshape(1), slopes])
        shared = (lower_bounds[l].reshape(1, WIDTH), attn_params, lam_init, _bf(w_in[l]),
                  hgrn_norm_g[l].reshape(1, HEAD_DIM), diff_norm_g[l].reshape(1, HEAD_DIM), _bf(w_out[l]),
                  ln1_g[l].reshape(1, D_MODEL), ln1_b[l].reshape(1, D_MODEL), _bf(peer_wq[l]),
                  _bf(peer_sub_keys[l].reshape(2 * PEER_HEADS, N_KEYS, HEAD_DIM)), _pack_tables(peer_u[l], peer_v[l]),
                  ln2_g[l].reshape(1, D_MODEL), ln2_b[l].reshape(1, D_MODEL))
        y_p, kp, vp, sp = _trunk_layer(y_p, None, None, None, *shared)
        y_s, kn, vn, sn = _trunk_layer(y_s, cache_k[l], cache_v[l], state_hgrn[l], *shared)
        for lst, val in zip(outs, (kp, vp, sp, kn, vn, sn)):
            lst.append(val)
    return (y_p, y_s) + tuple(o[0][None] if len(o) == 1 else jnp.stack(o) for o in outs)
```

```python
import functools
import math

import jax
import jax.numpy as jnp
from jax import lax
from jax.experimental import pallas as pl
from jax.experimental.pallas import tpu as pltpu

D_MODEL = 2048
CHUNK = 64
HEADS = 8
HEAD_DIM = 128
WIDTH = HEADS * HEAD_DIM
N_GROUPS = 7
B_DH = 64
PEER_HEADS = 8
N_KEYS = 128
PEER_TOPK = 16
PEER_SLOTS = PEER_HEADS * PEER_TOPK
TOKEN_BLOCK = 128
LN_EPS = 1e-5
RMS_EPS = 1e-5
DEPTH = 1
DEEPNORM_ALPHA = (2.0 * DEPTH) ** 0.25
ROWS, LANES = 8, 128
SUB = 16
LANE_GROUP = 512
VMEM_LIMIT = 48 * 1024 * 1024

_NT = (((1,), (1,)), ((), ()))
_TN = (((0,), (0,)), ((), ()))


def _bf(x):
    return x.astype(jnp.bfloat16)


def _dot(a, b):
    return jnp.dot(_bf(a), _bf(b), preferred_element_type=jnp.float32)


def _dot_nt(a, b):
    return lax.dot_general(_bf(a), _bf(b), _NT, preferred_element_type=jnp.float32)


def _dot_tn(a, b):
    return lax.dot_general(_bf(a), _bf(b), _TN, preferred_element_type=jnp.float32)


N_MIX_GROUPS = 5


def _in_proj_kernel(x_ref, w_ref, z_ref, k_ref, v_ref):
    j = pl.program_id(1)
    z = jnp.dot(_bf(x_ref[...]), w_ref[...], preferred_element_type=jnp.float32)

    @pl.when(j < N_MIX_GROUPS)
    def _():
        z_ref[...] = z

    @pl.when(j == N_MIX_GROUPS)
    def _():
        k_ref[...] = z

    @pl.when(j == N_MIX_GROUPS + 1)
    def _():
        v_ref[...] = z


def _in_proj(x2d, w_bf16):
    n = x2d.shape[0]
    tm = 512
    assert n % tm == 0
    return pl.pallas_call(
        _in_proj_kernel,
        grid=(n // tm, N_GROUPS),
        in_specs=[pl.BlockSpec((tm, D_MODEL), lambda i, j: (i, 0)),
                  pl.BlockSpec((D_MODEL, WIDTH), lambda i, j: (0, j))],
        out_specs=[pl.BlockSpec((None, tm, WIDTH), lambda i, j: (jnp.minimum(j, N_MIX_GROUPS - 1), i, 0)),
                   pl.BlockSpec((tm, WIDTH), lambda i, j: (i, 0)),
                   pl.BlockSpec((tm, WIDTH), lambda i, j: (i, 0))],
        out_shape=[jax.ShapeDtypeStruct((N_MIX_GROUPS, n, WIDTH), jnp.float32),
                   jax.ShapeDtypeStruct((n, WIDTH), jnp.float32),
                   jax.ShapeDtypeStruct((n, WIDTH), jnp.float32)],
        compiler_params=pltpu.CompilerParams(
            dimension_semantics=("parallel", "arbitrary"), vmem_limit_bytes=VMEM_LIMIT),
        name="in_proj",
    )(x2d, w_bf16)


def _hgrn_kernel(q_ref, fa_ref, v_ref, ga_ref, s0_ref, lb_ref, g_ref, tri_ref,
                 o_ref, sfin_ref, st_ref, b_ref, kk_ref, *, n_chunks):
    c = pl.program_id(1)

    @pl.when(c == 0)
    def _():
        for h in range(HEADS):
            st_ref[h] = s0_ref[h].T

    lb = lb_ref[...]
    f = lb + (1.0 - lb) * jax.nn.sigmoid(fa_ref[...])
    lf = jnp.log(f)
    kk_ref[...] = 1.0 - f
    lf_hi = _bf(lf)
    lf_lo = _bf(lf - lf_hi.astype(jnp.float32))
    tri = tri_ref[...]
    b_ref[...] = (jnp.dot(tri, lf_hi, preferred_element_type=jnp.float32)
                  + jnp.dot(tri, lf_lo, preferred_element_type=jnp.float32))

    for h in range(HEADS):
        hs = slice(h * HEAD_DIM, (h + 1) * HEAD_DIM)
        bh = b_ref[:, hs]
        qh = q_ref[:, hs]
        kh = kk_ref[:, hs]
        vh = v_ref[:, hs]
        st = st_ref[h]
        o_h = _dot_nt(qh * jnp.exp(bh), st)
        parts = [o_h[0:SUB]]
        for i in range(1, CHUNK // SUB):
            r0 = i * SUB
            b_start = bh[r0 - 1:r0]
            q_i = qh[r0:r0 + SUB] * jnp.exp(bh[r0:r0 + SUB] - b_start)
            k_p = kh[0:r0] * jnp.exp(b_start - bh[0:r0])
            sc = _dot_nt(q_i, k_p)
            parts.append(o_h[r0:r0 + SUB] + _dot(sc, vh[0:r0]))
        o_ref[:, hs] = jnp.concatenate(parts, axis=0)
        b_end = bh[CHUNK - 1:CHUNK]
        kd = kh * jnp.exp(b_end - bh)
        st_ref[h] = st * jnp.exp(b_end) + _dot_tn(vh, kd)

    row = lax.broadcasted_iota(jnp.int32, (SUB, LANE_GROUP), 0)
    for i in range(CHUNK // SUB):
        r0 = i * SUB
        for g0 in range(0, WIDTH, LANE_GROUP):
            ls = slice(g0, g0 + LANE_GROUP)
            b_i = b_ref[r0:r0 + SUB, ls]
            q_i = q_ref[r0:r0 + SUB, ls]

            def body(s, acc, r0=r0, ls=ls, b_i=b_i, q_i=q_i):
                b_s = b_ref[pl.ds(r0 + s, 1), ls]
                k_s = kk_ref[pl.ds(r0 + s, 1), ls]
                v_s = v_ref[pl.ds(r0 + s, 1), ls]
                e = jnp.exp(jnp.where(row >= s, b_i - b_s, -jnp.inf))
                p = q_i * (k_s * e)
                cols = []
                for h in range(LANE_GROUP // HEAD_DIM):
                    hs = slice(h * HEAD_DIM, (h + 1) * HEAD_DIM)
                    w = jnp.sum(p[:, hs], axis=1, keepdims=True)
                    cols.append(w * v_s[:, hs])
                return acc + jnp.concatenate(cols, axis=1)

            acc = lax.fori_loop(0, SUB, body, jnp.zeros((SUB, LANE_GROUP), jnp.float32), unroll=True)
            o_ref[r0:r0 + SUB, ls] += acc

    gate = jax.nn.sigmoid(ga_ref[...])
    gain = g_ref[...]
    for h in range(HEADS):
        hs = slice(h * HEAD_DIM, (h + 1) * HEAD_DIM)
        o_h = o_ref[:, hs]
        ms = jnp.mean(o_h * o_h, axis=1, keepdims=True)
        o_ref[:, hs] = o_h * lax.rsqrt(ms + RMS_EPS) * gain * gate[:, hs]

    @pl.when(c == n_chunks - 1)
    def _():
        for h in range(HEADS):
            sfin_ref[h] = st_ref[h].T


def _hgrn(z4, s0, lb, gain):
    _, batch, length, _ = z4.shape
    n_chunks = length // CHUNK
    tri = jnp.tril(jnp.ones((CHUNK, CHUNK), jnp.float32)).astype(jnp.bfloat16)

    def zspec(group):
        return pl.BlockSpec((None, None, CHUNK, WIDTH), lambda b, c, group=group: (group, b, c, 0))

    o, s_fin = pl.pallas_call(
        functools.partial(_hgrn_kernel, n_chunks=n_chunks),
        grid=(batch, n_chunks),
        in_specs=[zspec(0), zspec(1), zspec(2), zspec(3),
                  pl.BlockSpec((None, HEADS, HEAD_DIM, HEAD_DIM), lambda b, c: (b, 0, 0, 0)),
                  pl.BlockSpec((1, WIDTH), lambda b, c: (0, 0)),
                  pl.BlockSpec((1, HEAD_DIM), lambda b, c: (0, 0)),
                  pl.BlockSpec((CHUNK, CHUNK), lambda b, c: (0, 0))],
        out_specs=[pl.BlockSpec((None, CHUNK, WIDTH), lambda b, c: (b, c, 0)),
                   pl.BlockSpec((None, HEADS, HEAD_DIM, HEAD_DIM), lambda b, c: (b, 0, 0, 0))],
        out_shape=[jax.ShapeDtypeStruct((batch, length, WIDTH), jnp.float32),
                   jax.ShapeDtypeStruct((batch, HEADS, HEAD_DIM, HEAD_DIM), jnp.float32)],
        scratch_shapes=[pltpu.VMEM((HEADS, HEAD_DIM, HEAD_DIM), jnp.float32),
                        pltpu.VMEM((CHUNK, WIDTH), jnp.float32),
                        pltpu.VMEM((CHUNK, WIDTH), jnp.float32)],
        compiler_params=pltpu.CompilerParams(
            dimension_semantics=("parallel", "arbitrary"), vmem_limit_bytes=VMEM_LIMIT),
        name="hgrn2",
    )(z4, z4, z4, z4, s0, lb, gain, tri)
    return o.reshape(batch * length, WIDTH), s_fin


def _attn_init(q_ref, q1_ref, q2_ref, m_ref, l_ref, acc_ref):
    lane = lax.broadcasted_iota(jnp.int32, q_ref.shape, 1)
    q = q_ref[...] * (B_DH ** -0.5)
    q1_ref[...] = _bf(jnp.where(lane < B_DH, q, 0.0))
    q2_ref[...] = _bf(jnp.where(lane >= B_DH, q, 0.0))
    m_ref[...] = jnp.full(m_ref.shape, -jnp.inf, jnp.float32)
    l_ref[...] = jnp.zeros(l_ref.shape, jnp.float32)
    acc_ref[...] = jnp.zeros(acc_ref.shape, jnp.float32)


def _attn_update(bias, kb, vb, q1_ref, q2_ref, m_ref, l_ref, acc_ref):
    for m, qm_ref in enumerate((q1_ref, q2_ref)):
        s = lax.dot_general(qm_ref[...], kb, _NT, preferred_element_type=jnp.float32) + bias
        m_prev = m_ref[m]
        m_new = jnp.maximum(m_prev, jnp.max(s, axis=1, keepdims=True))
        alpha = jnp.exp(m_prev - m_new)
        p = jnp.exp(s - m_new)
        l_ref[m] = alpha * l_ref[m] + jnp.sum(p, axis=1, keepdims=True)
        acc_ref[m] = alpha * acc_ref[m] + jnp.dot(_bf(p), vb, preferred_element_type=jnp.float32)
        m_ref[m] = m_new


def _attn_finish(lam, lam_init, g_ref, o_ref, l_ref, acc_ref):
    o = acc_ref[0] / l_ref[0] - lam * (acc_ref[1] / l_ref[1])
    ms = jnp.mean(o * o, axis=1, keepdims=True)
    o_ref[...] = o * lax.rsqrt(ms + RMS_EPS) * g_ref[...] * (1.0 - lam_init)


def _attn_update_t(bias_t, kb, vb, q1_ref, q2_ref, m_ref, l_ref, acc_ref):
    for m, qm_ref in enumerate((q1_ref, q2_ref)):
        s = lax.dot_general(kb, qm_ref[...], _NT, preferred_element_type=jnp.float32) + bias_t
        m_prev = m_ref[m]
        m_new = jnp.maximum(m_prev, jnp.max(s, axis=0, keepdims=True))
        alpha = jnp.exp(m_prev - m_new)
        p = jnp.exp(s - m_new)
        l_ref[m] = alpha * l_ref[m] + jnp.sum(p, axis=0, keepdims=True)
        acc_ref[m] = alpha * acc_ref[m] + lax.dot_general(vb, _bf(p), _TN, preferred_element_type=jnp.float32)
        m_ref[m] = m_new


def _attn_prompt_kernel(qt_ref, kt_ref, par_ref, q_ref, k_ref, v_ref, pat_ref, g_ref, o_ref,
                        q1_ref, q2_ref, m_ref, l_ref, acc_ref, *, t, lam_init):
    h = pl.program_id(1)
    pair = pl.program_id(2)
    qi = qt_ref[pair]
    ki = kt_ref[pair]
    slope = par_ref[1 + h]

    @pl.when(ki == 0)
    def _():
        _attn_init(q_ref, q1_ref, q2_ref, m_ref, l_ref, acc_ref)

    kb = _bf(k_ref[...])
    vb = _bf(v_ref[...])

    @pl.when(ki < qi)
    def _():
        row = lax.broadcasted_iota(jnp.int32, (t, 1), 0)
        bias_t = slope * ((ki - qi) * t + row).astype(jnp.float32)
        _attn_update_t(bias_t, kb, vb, q1_ref, q2_ref, m_ref, l_ref, acc_ref)

    @pl.when(ki == qi)
    def _():
        _attn_update_t(slope * pat_ref[...], kb, vb, q1_ref, q2_ref, m_ref, l_ref, acc_ref)
        o_t = acc_ref[0] / l_ref[0] - par_ref[0] * (acc_ref[1] / l_ref[1])
        ms = jnp.mean(o_t * o_t, axis=0, keepdims=True)
        o_t = o_t * lax.rsqrt(ms + RMS_EPS) * g_ref[...] * (1.0 - lam_init)
        o_ref[...] = o_t.T


def _attn_cached_kernel(par_ref, q_ref, k_ref, v_ref, g_ref, o_ref,
                        q1_ref, q2_ref, m_ref, l_ref, acc_ref, *, qpos0, s_valid, lam_init):
    tq, tk = q_ref.shape[0], k_ref.shape[0]
    slope = par_ref[1 + pl.program_id(1)]
    _attn_init(q_ref, q1_ref, q2_ref, m_ref, l_ref, acc_ref)
    qpos = qpos0 + lax.broadcasted_iota(jnp.int32, (tq, tk), 0)
    kpos = lax.broadcasted_iota(jnp.int32, (tq, tk), 1)
    visible = ((kpos // CHUNK) <= (qpos // CHUNK)) & (kpos < s_valid)
    bias = jnp.where(visible, -slope * jnp.abs(qpos - kpos).astype(jnp.float32), -jnp.inf)
    _attn_update(bias, _bf(k_ref[...]), _bf(v_ref[...]), q1_ref, q2_ref, m_ref, l_ref, acc_ref)
    _attn_finish(par_ref[0], lam_init, g_ref, o_ref, l_ref, acc_ref)


def _attn_scratch(tq):
    return [pltpu.VMEM((tq, HEAD_DIM), jnp.bfloat16),
            pltpu.VMEM((tq, HEAD_DIM), jnp.bfloat16),
            pltpu.VMEM((2, tq, 1), jnp.float32),
            pltpu.VMEM((2, tq, 1), jnp.float32),
            pltpu.VMEM((2, tq, HEAD_DIM), jnp.float32)]


ATTN_BLOCK = 1024


Q_GROUP = 4


def _diff_attn_prompt(params, z4, k, v, gain, *, lam_init):
    _, batch, length, _ = z4.shape
    t = ATTN_BLOCK
    assert length % t == 0 and t % CHUNK == 0
    nb = length // t
    pairs = [(qi, ki) for qi in range(nb) for ki in range(qi + 1)]
    qt = jnp.array([p[0] for p in pairs], jnp.int32)
    kt = jnp.array([p[1] for p in pairs], jnp.int32)
    kk = lax.broadcasted_iota(jnp.int32, (t, t), 0)
    qq = lax.broadcasted_iota(jnp.int32, (t, t), 1)
    pattern = jnp.where(kk // CHUNK <= qq // CHUNK, jnp.minimum(kk, 2 * qq - kk).astype(jnp.float32), -jnp.inf)
    scratch = [pltpu.VMEM((t, HEAD_DIM), jnp.bfloat16),
               pltpu.VMEM((t, HEAD_DIM), jnp.bfloat16),
               pltpu.VMEM((2, 1, t), jnp.float32),
               pltpu.VMEM((2, 1, t), jnp.float32),
               pltpu.VMEM((2, HEAD_DIM, t), jnp.float32)]

    q_map = lambda b, h, p, qt, kt, par: (b, qt[p], h)
    k_map = lambda b, h, p, qt, kt, par: (b, kt[p], h)
    return pl.pallas_call(
        functools.partial(_attn_prompt_kernel, t=t, lam_init=lam_init),
        grid_spec=pltpu.PrefetchScalarGridSpec(
            num_scalar_prefetch=3,
            grid=(batch, HEADS, len(pairs)),
            in_specs=[pl.BlockSpec((None, None, t, HEAD_DIM), lambda b, h, p, qt, kt, par: (Q_GROUP, b, qt[p], h)),
                      pl.BlockSpec((None, t, HEAD_DIM), k_map),
                      pl.BlockSpec((None, t, HEAD_DIM), k_map),
                      pl.BlockSpec((t, t), lambda b, h, p, qt, kt, par: (0, 0)),
                      pl.BlockSpec((HEAD_DIM, 1), lambda b, h, p, qt, kt, par: (0, 0))],
            out_specs=pl.BlockSpec((None, t, HEAD_DIM), q_map),
            scratch_shapes=scratch),
        out_shape=jax.ShapeDtypeStruct((batch, length, WIDTH), jnp.float32),
        compiler_params=pltpu.CompilerParams(
            dimension_semantics=("parallel", "parallel", "arbitrary"), vmem_limit_bytes=VMEM_LIMIT),
        name="diff_attn_prompt",
    )(qt, kt, params, z4, k, v, pattern, gain.reshape(HEAD_DIM, 1))


def _diff_attn_cached(params, z4, k, v, gain, *, qpos0, s_valid, lam_init):
    _, batch, t, _ = z4.shape
    s = k.shape[1]
    return pl.pallas_call(
        functools.partial(_attn_cached_kernel, qpos0=qpos0, s_valid=s_valid, lam_init=lam_init),
        grid_spec=pltpu.PrefetchScalarGridSpec(
            num_scalar_prefetch=1,
            grid=(batch, HEADS),
            in_specs=[pl.BlockSpec((None, None, t, HEAD_DIM), lambda b, h, par: (Q_GROUP, b, 0, h)),
                      pl.BlockSpec((None, s, HEAD_DIM), lambda b, h, par: (b, 0, h)),
                      pl.BlockSpec((None, s, HEAD_DIM), lambda b, h, par: (b, 0, h)),
                      pl.BlockSpec((1, HEAD_DIM), lambda b, h, par: (0, 0))],
            out_specs=pl.BlockSpec((None, t, HEAD_DIM), lambda b, h, par: (b, 0, h)),
            scratch_shapes=_attn_scratch(t)),
        out_shape=jax.ShapeDtypeStruct((batch, t, WIDTH), jnp.float32),
        compiler_params=pltpu.CompilerParams(
            dimension_semantics=("parallel", "parallel"), vmem_limit_bytes=VMEM_LIMIT),
        name="diff_attn_cached",
    )(params, z4, k, v, gain)


def _layer_norm(y, g, b):
    mu = jnp.mean(y, axis=1, keepdims=True)
    d = y - mu
    var = jnp.mean(d * d, axis=1, keepdims=True)
    return d * lax.rsqrt(var + LN_EPS) * g + b


def _out_proj_kernel(oa_ref, ob_ref, x_ref, w_ref, g_ref, b_ref, o_ref):
    acc = jnp.dot(_bf(oa_ref[...]), w_ref[0:WIDTH, :], preferred_element_type=jnp.float32)
    acc += jnp.dot(_bf(ob_ref[...]), w_ref[WIDTH:2 * WIDTH, :], preferred_element_type=jnp.float32)
    o_ref[...] = _layer_norm(DEEPNORM_ALPHA * x_ref[...] + acc, g_ref[...], b_ref[...])


def _out_proj(oa, ob, x2d, w_bf16, g, b):
    n = x2d.shape[0]
    tm = 256
    assert n % tm == 0
    row = lambda i: (i, 0)
    const = lambda i: (0, 0)
    return pl.pallas_call(
        _out_proj_kernel,
        grid=(n // tm,),
        in_specs=[pl.BlockSpec((tm, WIDTH), row), pl.BlockSpec((tm, WIDTH), row),
                  pl.BlockSpec((tm, D_MODEL), row), pl.BlockSpec((2 * WIDTH, D_MODEL), const),
                  pl.BlockSpec((1, D_MODEL), const), pl.BlockSpec((1, D_MODEL), const)],
        out_specs=pl.BlockSpec((tm, D_MODEL), row),
        out_shape=jax.ShapeDtypeStruct((n, D_MODEL), jnp.float32),
        compiler_params=pltpu.CompilerParams(dimension_semantics=("parallel",), vmem_limit_bytes=VMEM_LIMIT),
        name="out_proj",
    )(oa, ob, x2d, w_bf16, g, b)


def _top16(vals, pos, ids=None):
    top_v, top_i = [], []
    for _ in range(PEER_TOPK):
        m = jnp.max(vals, axis=0, keepdims=True)
        first = jnp.min(jnp.where(vals == m, pos, jnp.inf), axis=0, keepdims=True)
        hit = pos == first
        top_v.append(m)
        top_i.append(first if ids is None else jnp.max(jnp.where(hit, ids, -1.0), axis=0, keepdims=True))
        vals = jnp.where(hit, -jnp.inf, vals)
    return jnp.concatenate(top_v, axis=0), jnp.concatenate(top_i, axis=0)


def _pair_candidates(sv, si):
    t = sv[0].shape[1]
    row8 = lax.broadcasted_iota(jnp.int32, (ROWS, t), 0).astype(jnp.float32)
    row16 = lax.broadcasted_iota(jnp.int32, (PEER_TOPK, t), 0).astype(jnp.float32)
    vals = [sv[0][0:1] + sv[1]]
    pos = [row16]
    ids = [si[0][0:1] * N_KEYS + si[1]]
    for a in range(1, ROWS):
        n_b = PEER_TOPK // (a + 1)
        vals.append(jnp.where(row8 < n_b, sv[0][a:a + 1] + sv[1][0:ROWS], -jnp.inf))
        pos.append(row8 + float(a * PEER_TOPK))
        ids.append(si[0][a:a + 1] * N_KEYS + si[1][0:ROWS])
    vals.append(sv[0][ROWS:] + sv[1][0:1])
    pos.append((row8 + float(ROWS)) * float(PEER_TOPK))
    ids.append(si[0][ROWS:] * N_KEYS + si[1][0:1])
    return jnp.concatenate(vals, axis=0), jnp.concatenate(pos, axis=0), jnp.concatenate(ids, axis=0)


def _peer_topk_kernel(x_ref, wq_ref, sk_ref, idx_ref, gate_ref):
    q = jnp.dot(_bf(x_ref[...]), wq_ref[...], preferred_element_type=jnp.float32)
    key_id = lax.broadcasted_iota(jnp.int32, (N_KEYS, TOKEN_BLOCK), 0).astype(jnp.float32)
    for h in range(PEER_HEADS):
        sv, si = [], []
        for c in range(2):
            col = (2 * h + c) * HEAD_DIM
            s_t = _dot_nt(sk_ref[2 * h + c], q[:, col:col + HEAD_DIM])
            v16, i16 = _top16(s_t, key_id)
            sv.append(v16)
            si.append(i16)
        cv, eidx = _top16(*_pair_candidates(sv, si))
        e = jnp.exp(cv - cv[0:1])
        rows = slice(h * PEER_TOPK, (h + 1) * PEER_TOPK)
        gate_ref[0, rows, :] = e / jnp.sum(e, axis=0, keepdims=True)
        idx_ref[0, rows, :] = eidx.astype(jnp.int32)


def _peer_topk(x2d, wq_bf16, sk_bf16):
    n = x2d.shape[0]
    nb = n // TOKEN_BLOCK
    return pl.pallas_call(
        _peer_topk_kernel,
        grid=(nb,),
        in_specs=[pl.BlockSpec((TOKEN_BLOCK, D_MODEL), lambda i: (i, 0)),
                  pl.BlockSpec((D_MODEL, D_MODEL), lambda i: (0, 0)),
                  pl.BlockSpec((2 * PEER_HEADS, N_KEYS, HEAD_DIM), lambda i: (0, 0, 0))],
        out_specs=[pl.BlockSpec((1, PEER_SLOTS, TOKEN_BLOCK), lambda i: (i, 0, 0)),
                   pl.BlockSpec((1, PEER_SLOTS, TOKEN_BLOCK), lambda i: (i, 0, 0))],
        out_shape=[jax.ShapeDtypeStruct((nb, PEER_SLOTS, TOKEN_BLOCK), jnp.int32),
                   jax.ShapeDtypeStruct((nb, PEER_SLOTS, TOKEN_BLOCK), jnp.float32)],
        compiler_params=pltpu.CompilerParams(dimension_semantics=("parallel",), vmem_limit_bytes=VMEM_LIMIT),
        name="peer_topk",
    )(x2d, wq_bf16, sk_bf16)


PEER_TB = 32
PEER_RING = 16
PEER_AHEAD = 8


def _pack_tables(u, v):
    ub = lax.bitcast_convert_type(_bf(u), jnp.uint16).astype(jnp.uint32)
    vb = lax.bitcast_convert_type(_bf(v), jnp.uint16).astype(jnp.uint32)
    return ((ub << 16) | vb).reshape(u.shape[0], 1, u.shape[1])


def _peer_mix_kernel(idx_ref, idx_next_ref, gate_ref, x_ref, tab_hbm, g_ref, b_ref, o_ref,
                     buf, xb_ref, y_ref, sem, *, steps):
    i = pl.program_id(0)
    groups = PEER_SLOTS // ROWS
    per_group = PEER_SLOTS // (2 * groups)

    def start_rows(ids_ref, tok, j0, j1):
        slot = tok % PEER_RING
        for j in range(j0, j1):
            e = ids_ref[0, 0, tok * PEER_SLOTS + j]
            pltpu.make_async_copy(tab_hbm.at[e], buf.at[slot, pl.ds(j, 1)], sem.at[slot]).start(priority=j % 2)

    def wait_token(tok):
        slot = tok % PEER_RING
        pltpu.make_async_copy(tab_hbm.at[pl.ds(0, PEER_SLOTS), 0], buf.at[slot], sem.at[slot]).wait()

    def ahead(t):
        nxt = t + PEER_AHEAD
        return (idx_ref, nxt) if nxt < PEER_TB else (idx_next_ref, nxt - PEER_TB)

    def u_phase(t, req):
        wait_token(t)
        parts = []
        for g in range(groups):
            start_rows(*req, per_group * g, per_group * (g + 1))
            r0 = ROWS * g
            acc = None
            for c0 in range(0, D_MODEL, LANES):
                words = buf[t % PEER_RING, r0:r0 + ROWS, c0:c0 + LANES]
                term = (lax.bitcast_convert_type(words & jnp.uint32(0xFFFF0000), jnp.float32)
                        * xb_ref[t, :, c0:c0 + LANES])
                acc = term if acc is None else acc + term
            parts.append(acc)
        act = jnp.sum(jnp.concatenate(parts, axis=0), axis=1, keepdims=True)
        act = 0.5 * act * (1.0 + lax.erf(act * (2.0 ** -0.5)))
        return gate_ref[0, :, t:t + 1] * act

    def v_phase(t, w, req):
        y_acc = jnp.zeros((ROWS, D_MODEL), jnp.float32)
        half = PEER_SLOTS // 2
        for g in range(groups):
            if req is not None:
                start_rows(*req, half + per_group * g, half + per_group * (g + 1))
            r0 = ROWS * g
            words = buf[t % PEER_RING, r0:r0 + ROWS, :]
            y_acc = y_acc + lax.bitcast_convert_type(words << 16, jnp.float32) * w[r0:r0 + ROWS]
        y_ref[t:t + 1, :] = jnp.sum(y_acc, axis=0, keepdims=True)

    @pl.when(i == 0)
    def _():
        for t in range(PEER_AHEAD):
            start_rows(idx_ref, t, 0, PEER_SLOTS)

    for t in range(PEER_TB):
        xb_ref[t] = jnp.broadcast_to(x_ref[t:t + 1, :], (ROWS, D_MODEL))

    w_prev = None
    for t in range(PEER_TB):
        w = u_phase(t, ahead(t))
        if t == 0:
            start_rows(*ahead(0), PEER_SLOTS // 2, PEER_SLOTS)
        else:
            v_phase(t - 1, w_prev, ahead(t))
        w_prev = w
    v_phase(PEER_TB - 1, w_prev, None)

    o_ref[...] = _layer_norm(DEEPNORM_ALPHA * x_ref[...] + y_ref[...], g_ref[...], b_ref[...])

    @pl.when(i == steps - 1)
    def _():
        for t in range(PEER_AHEAD):
            wait_token(t)


def _peer_mix(idx, gate, x2d, table, g, b):
    n = x2d.shape[0]
    steps = n // PEER_TB
    assert n % PEER_TB == 0 and PEER_TB % PEER_RING == 0 and PEER_AHEAD <= PEER_RING - 2
    idx_tm = jnp.swapaxes(idx, 1, 2).reshape(steps, 1, PEER_TB * PEER_SLOTS)
    gate_tm = jnp.swapaxes(jnp.swapaxes(gate, 1, 2).reshape(steps, PEER_TB, PEER_SLOTS), 1, 2)
    ids_block = (1, 1, PEER_TB * PEER_SLOTS)
    return pl.pallas_call(
        functools.partial(_peer_mix_kernel, steps=steps),
        grid=(steps,),
        in_specs=[pl.BlockSpec(ids_block, lambda i: (i, 0, 0), memory_space=pltpu.SMEM),
                  pl.BlockSpec(ids_block, lambda i: (jnp.minimum(i + 1, steps - 1), 0, 0), memory_space=pltpu.SMEM),
                  pl.BlockSpec((1, PEER_SLOTS, PEER_TB), lambda i: (i, 0, 0)),
                  pl.BlockSpec((PEER_TB, D_MODEL), lambda i: (i, 0)),
                  pl.BlockSpec(memory_space=pl.ANY),
                  pl.BlockSpec((1, D_MODEL), lambda i: (0, 0)),
                  pl.BlockSpec((1, D_MODEL), lambda i: (0, 0))],
        out_specs=pl.BlockSpec((PEER_TB, D_MODEL), lambda i: (i, 0)),
        out_shape=jax.ShapeDtypeStruct((n, D_MODEL), jnp.float32),
        scratch_shapes=[pltpu.VMEM((PEER_RING, PEER_SLOTS, D_MODEL), jnp.uint32),
                        pltpu.VMEM((PEER_TB, ROWS, D_MODEL), jnp.float32),
                        pltpu.VMEM((PEER_TB, D_MODEL), jnp.float32),
                        pltpu.SemaphoreType.DMA((PEER_RING,))],
        compiler_params=pltpu.CompilerParams(dimension_semantics=("arbitrary",), vmem_limit_bytes=VMEM_LIMIT),
        name="peer_mix",
    )(idx_tm, idx_tm, gate_tm, x2d, table, g, b)


def _trunk_layer(x, past_k, past_v, s0, lb, attn_params, lam_init, w_in, a_gain, b_gain, w_out,
                 ln1_g, ln1_b, wq, sub_keys, table, ln2_g, ln2_b):
    batch, length, _ = x.shape
    n = batch * length
    x2d = x.reshape(n, D_MODEL)
    z, k2d, v2d = _in_proj(x2d, w_in)
    z4 = z.reshape(N_MIX_GROUPS, batch, length, WIDTH)
    k3d = k2d.reshape(batch, length, WIDTH)
    v3d = v2d.reshape(batch, length, WIDTH)

    if s0 is None:
        s0 = jnp.zeros((batch, HEADS, HEAD_DIM, HEAD_DIM), jnp.float32)
    o_a, s_new = _hgrn(z4, s0, lb, a_gain)

    if past_k is None:
        o_b = _diff_attn_prompt(attn_params, z4, k3d, v3d, b_gain, lam_init=lam_init)
    else:
        past = past_k.shape[1]
        total = past + length
        padded = -(-total // HEAD_DIM) * HEAD_DIM
        pad = jnp.zeros((batch, padded - total, WIDTH), jnp.float32)
        k_all = jnp.concatenate([past_k.reshape(batch, past, WIDTH), k3d, pad], axis=1)
        v_all = jnp.concatenate([past_v.reshape(batch, past, WIDTH), v3d, pad], axis=1)
        o_b = _diff_attn_cached(attn_params, z4, k_all, v_all, b_gain, qpos0=past, s_valid=total, lam_init=lam_init)
    k_new = k2d.reshape(batch, length, HEADS, HEAD_DIM)
    v_new = v2d.reshape(batch, length, HEADS, HEAD_DIM)

    x1 = _out_proj(o_a, o_b.reshape(n, WIDTH), x2d, w_out, ln1_g, ln1_b)
    idx, gate = _peer_topk(x1, wq, sub_keys)
    y = _peer_mix(idx, gate, x1, table, ln2_g, ln2_b)
    return y.reshape(batch, length, D_MODEL), k_new, v_new, s_new


def kernel(x_prompt, x_sample, cache_k, cache_v, state_hgrn, w_in, hgrn_lb, hgrn_norm_g, diff_lq1, diff_lk1,
           diff_lq2, diff_lk2, diff_norm_g, w_out, ln1_g, ln1_b, peer_wq, peer_sub_keys, peer_u, peer_v,
           ln2_g, ln2_b):
    f32 = jnp.float32
    lower_bounds = jnp.cumsum(jax.nn.softmax(hgrn_lb.astype(f32), axis=0), axis=0)
    slopes = 2.0 ** (-8.0 * jnp.arange(1, HEADS + 1, dtype=f32) / HEADS)
    y_p, y_s = x_prompt, x_sample
    outs = [[] for _ in range(6)]
    for l in range(DEPTH):
        lam_init = 0.8 - 0.6 * math.exp(-0.3 * l)
        lam = (jnp.exp(jnp.sum(diff_lq1[l].astype(f32) * diff_lk1[l].astype(f32)))
               - jnp.exp(jnp.sum(diff_lq2[l].astype(f32) * diff_lk2[l].astype(f32))) + lam_init)
        attn_params = jnp.concatenate([lam.reshape(1), slopes])
        shared = (lower_bounds[l].reshape(1, WIDTH), attn_params, lam_init, _bf(w_in[l]),
                  hgrn_norm_g[l].reshape(1, HEAD_DIM), diff_norm_g[l].reshape(1, HEAD_DIM), _bf(w_out[l]),
                  ln1_g[l].reshape(1, D_MODEL), ln1_b[l].reshape(1, D_MODEL), _bf(peer_wq[l]),
                  _bf(peer_sub_keys[l].reshape(2 * PEER_HEADS, N_KEYS, HEAD_DIM)), _pack_tables(peer_u[l], peer_v[l]),
                  ln2_g[l].reshape(1, D_MODEL), ln2_b[l].reshape(1, D_MODEL))
        y_p, kp, vp, sp = _trunk_layer(y_p, None, None, None, *shared)
        y_s, kn, vn, sn = _trunk_layer(y_s, cache_k[l], cache_v[l], state_hgrn[l], *shared)
        for lst, val in zip(outs, (kp, vp, sp, kn, vn, sn)):
            lst.append(val)
    return (y_p, y_s) + tuple(o[0][None] if len(o) == 1 else jnp.stack(o) for o in outs)
```

```python
import functools
import math

import jax
import jax.numpy as jnp
from jax import lax
from jax.experimental import pallas as pl
from jax.experimental.pallas import tpu as pltpu

D_MODEL = 2048
CHUNK = 64
HEADS = 8
HEAD_DIM = 128
WIDTH = HEADS * HEAD_DIM
N_GROUPS = 7
B_DH = 64
PEER_HEADS = 8
N_KEYS = 128
PEER_TOPK = 16
PEER_SLOTS = PEER_HEADS * PEER_TOPK
TOKEN_BLOCK = 128
LN_EPS = 1e-5
RMS_EPS = 1e-5
DEPTH = 1
DEEPNORM_ALPHA = (2.0 * DEPTH) ** 0.25
ROWS, LANES = 8, 128
SUB = 16
LANE_GROUP = 512
VMEM_LIMIT = 48 * 1024 * 1024

_NT = (((1,), (1,)), ((), ()))
_TN = (((0,), (0,)), ((), ()))


def _bf(x):
    return x.astype(jnp.bfloat16)


def _dot(a, b):
    return jnp.dot(_bf(a), _bf(b), preferred_element_type=jnp.float32)


def _dot_nt(a, b):
    return lax.dot_general(_bf(a), _bf(b), _NT, preferred_element_type=jnp.float32)


def _dot_tn(a, b):
    return lax.dot_general(_bf(a), _bf(b), _TN, preferred_element_type=jnp.float32)


N_MIX_GROUPS = 5


def _in_proj_kernel(x_ref, w_ref, z_ref, k_ref, v_ref):
    j = pl.program_id(1)
    z = jnp.dot(_bf(x_ref[...]), w_ref[...], preferred_element_type=jnp.float32)

    @pl.when(j < N_MIX_GROUPS)
    def _():
        z_ref[...] = z

    @pl.when(j == N_MIX_GROUPS)
    def _():
        k_ref[...] = z

    @pl.when(j == N_MIX_GROUPS + 1)
    def _():
        v_ref[...] = z


def _in_proj(x2d, w_bf16):
    n = x2d.shape[0]
    tm = 512
    assert n % tm == 0
    return pl.pallas_call(
        _in_proj_kernel,
        grid=(n // tm, N_GROUPS),
        in_specs=[pl.BlockSpec((tm, D_MODEL), lambda i, j: (i, 0)),
                  pl.BlockSpec((D_MODEL, WIDTH), lambda i, j: (0, j))],
        out_specs=[pl.BlockSpec((None, tm, WIDTH), lambda i, j: (jnp.minimum(j, N_MIX_GROUPS - 1), i, 0)),
                   pl.BlockSpec((tm, WIDTH), lambda i, j: (i, 0)),
                   pl.BlockSpec((tm, WIDTH), lambda i, j: (i, 0))],
        out_shape=[jax.ShapeDtypeStruct((N_MIX_GROUPS, n, WIDTH), jnp.float32),
                   jax.ShapeDtypeStruct((n, WIDTH), jnp.float32),
                   jax.ShapeDtypeStruct((n, WIDTH), jnp.float32)],
        compiler_params=pltpu.CompilerParams(
            dimension_semantics=("parallel", "arbitrary"), vmem_limit_bytes=VMEM_LIMIT),
        name="in_proj",
    )(x2d, w_bf16)


def _hgrn_kernel(q_ref, fa_ref, v_ref, ga_ref, s0_ref, lb_ref, g_ref, tri_ref,
                 o_ref, sfin_ref, st_ref, b_ref, kk_ref, *, n_chunks):
    c = pl.program_id(1)

    @pl.when(c == 0)
    def _():
        for h in range(HEADS):
            st_ref[h] = s0_ref[h].T

    lb = lb_ref[...]
    f = lb + (1.0 - lb) * jax.nn.sigmoid(fa_ref[...])
    lf = jnp.log(f)
    kk_ref[...] = 1.0 - f
    lf_hi = _bf(lf)
    lf_lo = _bf(lf - lf_hi.astype(jnp.float32))
    tri = tri_ref[...]
    b_ref[...] = (jnp.dot(tri, lf_hi, preferred_element_type=jnp.float32)
                  + jnp.dot(tri, lf_lo, preferred_element_type=jnp.float32))

    for h in range(HEADS):
        hs = slice(h * HEAD_DIM, (h + 1) * HEAD_DIM)
        bh = b_ref[:, hs]
        qh = q_ref[:, hs]
        kh = kk_ref[:, hs]
        vh = v_ref[:, hs]
        st = st_ref[h]
        o_h = _dot_nt(qh * jnp.exp(bh), st)
        parts = [o_h[0:SUB]]
        for i in range(1, CHUNK // SUB):
            r0 = i * SUB
            b_start = bh[r0 - 1:r0]
            q_i = qh[r0:r0 + SUB] * jnp.exp(bh[r0:r0 + SUB] - b_start)
            k_p = kh[0:r0] * jnp.exp(b_start - bh[0:r0])
            sc = _dot_nt(q_i, k_p)
            parts.append(o_h[r0:r0 + SUB] + _dot(sc, vh[0:r0]))
        o_ref[:, hs] = jnp.concatenate(parts, axis=0)
        b_end = bh[CHUNK - 1:CHUNK]
        kd = kh * jnp.exp(b_end - bh)
        st_ref[h] = st * jnp.exp(b_end) + _dot_tn(vh, kd)

    row = lax.broadcasted_iota(jnp.int32, (SUB, LANE_GROUP), 0)
    for i in range(CHUNK // SUB):
        r0 = i * SUB
        for g0 in range(0, WIDTH, LANE_GROUP):
            ls = slice(g0, g0 + LANE_GROUP)
            b_i = b_ref[r0:r0 + SUB, ls]
            q_i = q_ref[r0:r0 + SUB, ls]

            def body(s, acc, r0=r0, ls=ls, b_i=b_i, q_i=q_i):
                b_s = b_ref[pl.ds(r0 + s, 1), ls]
                k_s = kk_ref[pl.ds(r0 + s, 1), ls]
                v_s = v_ref[pl.ds(r0 + s, 1), ls]
                e = jnp.exp(jnp.where(row >= s, b_i - b_s, -jnp.inf))
                p = q_i * (k_s * e)
                cols = []
                for h in range(LANE_GROUP // HEAD_DIM):
                    hs = slice(h * HEAD_DIM, (h + 1) * HEAD_DIM)
                    w = jnp.sum(p[:, hs], axis=1, keepdims=True)
                    cols.append(w * v_s[:, hs])
                return acc + jnp.concatenate(cols, axis=1)

            acc = lax.fori_loop(0, SUB, body, jnp.zeros((SUB, LANE_GROUP), jnp.float32), unroll=True)
            o_ref[r0:r0 + SUB, ls] += acc

    gate = jax.nn.sigmoid(ga_ref[...])
    gain = g_ref[...]
    for h in range(HEADS):
        hs = slice(h * HEAD_DIM, (h + 1) * HEAD_DIM)
        o_h = o_ref[:, hs]
        ms = jnp.mean(o_h * o_h, axis=1, keepdims=True)
        o_ref[:, hs] = o_h * lax.rsqrt(ms + RMS_EPS) * gain * gate[:, hs]

    @pl.when(c == n_chunks - 1)
    def _():
        for h in range(HEADS):
            sfin_ref[h] = st_ref[h].T


def _hgrn(z4, s0, lb, gain):
    _, batch, length, _ = z4.shape
    n_chunks = length // CHUNK
    tri = jnp.tril(jnp.ones((CHUNK, CHUNK), jnp.float32)).astype(jnp.bfloat16)

    def zspec(group):
        return pl.BlockSpec((None, None, CHUNK, WIDTH), lambda b, c, group=group: (group, b, c, 0))

    o, s_fin = pl.pallas_call(
        functools.partial(_hgrn_kernel, n_chunks=n_chunks),
        grid=(batch, n_chunks),
        in_specs=[zspec(0), zspec(1), zspec(2), zspec(3),
                  pl.BlockSpec((None, HEADS, HEAD_DIM, HEAD_DIM), lambda b, c: (b, 0, 0, 0)),
                  pl.BlockSpec((1, WIDTH), lambda b, c: (0, 0)),
                  pl.BlockSpec((1, HEAD_DIM), lambda b, c: (0, 0)),
                  pl.BlockSpec((CHUNK, CHUNK), lambda b, c: (0, 0))],
        out_specs=[pl.BlockSpec((None, CHUNK, WIDTH), lambda b, c: (b, c, 0)),
                   pl.BlockSpec((None, HEADS, HEAD_DIM, HEAD_DIM), lambda b, c: (b, 0, 0, 0))],
        out_shape=[jax.ShapeDtypeStruct((batch, length, WIDTH), jnp.float32),
                   jax.ShapeDtypeStruct((batch, HEADS, HEAD_DIM, HEAD_DIM), jnp.float32)],
        scratch_shapes=[pltpu.VMEM((HEADS, HEAD_DIM, HEAD_DIM), jnp.float32),
                        pltpu.VMEM((CHUNK, WIDTH), jnp.float32),
                        pltpu.VMEM((CHUNK, WIDTH), jnp.float32)],
        compiler_params=pltpu.CompilerParams(
            dimension_semantics=("parallel", "arbitrary"), vmem_limit_bytes=VMEM_LIMIT),
        name="hgrn2",
    )(z4, z4, z4, z4, s0, lb, gain, tri)
    return o.reshape(batch * length, WIDTH), s_fin


def _attn_init(q_ref, q1_ref, q2_ref, m_ref, l_ref, acc_ref):
    lane = lax.broadcasted_iota(jnp.int32, q_ref.shape, 1)
    q = q_ref[...] * (B_DH ** -0.5)
    q1_ref[...] = _bf(jnp.where(lane < B_DH, q, 0.0))
    q2_ref[...] = _bf(jnp.where(lane >= B_DH, q, 0.0))
    m_ref[...] = jnp.full(m_ref.shape, -jnp.inf, jnp.float32)
    l_ref[...] = jnp.zeros(l_ref.shape, jnp.float32)
    acc_ref[...] = jnp.zeros(acc_ref.shape, jnp.float32)


def _attn_update(bias, kb, vb, q1_ref, q2_ref, m_ref, l_ref, acc_ref):
    for m, qm_ref in enumerate((q1_ref, q2_ref)):
        s = lax.dot_general(qm_ref[...], kb, _NT, preferred_element_type=jnp.float32) + bias
        m_prev = m_ref[m]
        m_new = jnp.maximum(m_prev, jnp.max(s, axis=1, keepdims=True))
        alpha = jnp.exp(m_prev - m_new)
        p = jnp.exp(s - m_new)
        l_ref[m] = alpha * l_ref[m] + jnp.sum(p, axis=1, keepdims=True)
        acc_ref[m] = alpha * acc_ref[m] + jnp.dot(_bf(p), vb, preferred_element_type=jnp.float32)
        m_ref[m] = m_new


def _attn_finish(lam, lam_init, g_ref, o_ref, l_ref, acc_ref):
    o = acc_ref[0] / l_ref[0] - lam * (acc_ref[1] / l_ref[1])
    ms = jnp.mean(o * o, axis=1, keepdims=True)
    o_ref[...] = o * lax.rsqrt(ms + RMS_EPS) * g_ref[...] * (1.0 - lam_init)


def _attn_update_t(bias_t, kb, vb, q1_ref, q2_ref, m_ref, l_ref, acc_ref, qs=slice(None)):
    for m, qm_ref in enumerate((q1_ref, q2_ref)):
        s = lax.dot_general(kb, qm_ref[qs, :], _NT, preferred_element_type=jnp.float32) + bias_t
        m_prev = m_ref[m, :, qs]
        m_new = jnp.maximum(m_prev, jnp.max(s, axis=0, keepdims=True))
        alpha = jnp.exp(m_prev - m_new)
        p = jnp.exp(s - m_new)
        l_ref[m, :, qs] = alpha * l_ref[m, :, qs] + jnp.sum(p, axis=0, keepdims=True)
        acc_ref[m, :, qs] = (alpha * acc_ref[m, :, qs]
                             + lax.dot_general(vb, _bf(p), _TN, preferred_element_type=jnp.float32))
        m_ref[m, :, qs] = m_new


def _attn_prompt_kernel(qt_ref, kt_ref, par_ref, q_ref, k_ref, v_ref, pat_ref, g_ref, o_ref,
                        q1_ref, q2_ref, m_ref, l_ref, acc_ref, *, t, lam_init):
    h = pl.program_id(1)
    pair = pl.program_id(2)
    qi = qt_ref[pair]
    ki = kt_ref[pair]
    slope = par_ref[1 + h]

    @pl.when(ki == 0)
    def _():
        _attn_init(q_ref, q1_ref, q2_ref, m_ref, l_ref, acc_ref)

    kb = _bf(k_ref[...])
    vb = _bf(v_ref[...])

    @pl.when(ki < qi)
    def _():
        row = lax.broadcasted_iota(jnp.int32, (t, 1), 0)
        bias_t = slope * ((ki - qi) * t + row).astype(jnp.float32)
        _attn_update_t(bias_t, kb, vb, q1_ref, q2_ref, m_ref, l_ref, acc_ref)

    @pl.when(ki == qi)
    def _():
        half = t // 2
        state = (q1_ref, q2_ref, m_ref, l_ref, acc_ref)
        _attn_update_t(slope * pat_ref[0:half, :], kb[0:half], vb[0:half], *state)
        _attn_update_t(slope * pat_ref[half:t, half:t], kb[half:t], vb[half:t], *state, qs=slice(half, t))
        o_t = acc_ref[0] / l_ref[0] - par_ref[0] * (acc_ref[1] / l_ref[1])
        ms = jnp.mean(o_t * o_t, axis=0, keepdims=True)
        o_t = o_t * lax.rsqrt(ms + RMS_EPS) * g_ref[...] * (1.0 - lam_init)
        o_ref[...] = o_t.T


def _attn_cached_kernel(par_ref, q_ref, k_ref, v_ref, g_ref, o_ref,
                        q1_ref, q2_ref, m_ref, l_ref, acc_ref, *, qpos0, s_valid, lam_init):
    tq, tk = q_ref.shape[0], k_ref.shape[0]
    slope = par_ref[1 + pl.program_id(1)]
    _attn_init(q_ref, q1_ref, q2_ref, m_ref, l_ref, acc_ref)
    qpos = qpos0 + lax.broadcasted_iota(jnp.int32, (tq, tk), 0)
    kpos = lax.broadcasted_iota(jnp.int32, (tq, tk), 1)
    visible = ((kpos // CHUNK) <= (qpos // CHUNK)) & (kpos < s_valid)
    bias = jnp.where(visible, -slope * jnp.abs(qpos - kpos).astype(jnp.float32), -jnp.inf)
    _attn_update(bias, _bf(k_ref[...]), _bf(v_ref[...]), q1_ref, q2_ref, m_ref, l_ref, acc_ref)
    _attn_finish(par_ref[0], lam_init, g_ref, o_ref, l_ref, acc_ref)


def _attn_scratch(tq):
    return [pltpu.VMEM((tq, HEAD_DIM), jnp.bfloat16),
            pltpu.VMEM((tq, HEAD_DIM), jnp.bfloat16),
            pltpu.VMEM((2, tq, 1), jnp.float32),
            pltpu.VMEM((2, tq, 1), jnp.float32),
            pltpu.VMEM((2, tq, HEAD_DIM), jnp.float32)]


ATTN_BLOCK = 1024


Q_GROUP = 4


def _diff_attn_prompt(params, z4, k, v, gain, *, lam_init):
    _, batch, length, _ = z4.shape
    t = ATTN_BLOCK
    assert length % t == 0 and t % CHUNK == 0
    nb = length // t
    pairs = [(qi, ki) for qi in range(nb) for ki in range(qi + 1)]
    qt = jnp.array([p[0] for p in pairs], jnp.int32)
    kt = jnp.array([p[1] for p in pairs], jnp.int32)
    kk = lax.broadcasted_iota(jnp.int32, (t, t), 0)
    qq = lax.broadcasted_iota(jnp.int32, (t, t), 1)
    pattern = jnp.where(kk // CHUNK <= qq // CHUNK, jnp.minimum(kk, 2 * qq - kk).astype(jnp.float32), -jnp.inf)
    scratch = [pltpu.VMEM((t, HEAD_DIM), jnp.bfloat16),
               pltpu.VMEM((t, HEAD_DIM), jnp.bfloat16),
               pltpu.VMEM((2, 1, t), jnp.float32),
               pltpu.VMEM((2, 1, t), jnp.float32),
               pltpu.VMEM((2, HEAD_DIM, t), jnp.float32)]

    q_map = lambda b, h, p, qt, kt, par: (b, qt[p], h)
    k_map = lambda b, h, p, qt, kt, par: (b, kt[p], h)
    return pl.pallas_call(
        functools.partial(_attn_prompt_kernel, t=t, lam_init=lam_init),
        grid_spec=pltpu.PrefetchScalarGridSpec(
            num_scalar_prefetch=3,
            grid=(batch, HEADS, len(pairs)),
            in_specs=[pl.BlockSpec((None, None, t, HEAD_DIM), lambda b, h, p, qt, kt, par: (Q_GROUP, b, qt[p], h)),
                      pl.BlockSpec((None, t, HEAD_DIM), k_map),
                      pl.BlockSpec((None, t, HEAD_DIM), k_map),
                      pl.BlockSpec((t, t), lambda b, h, p, qt, kt, par: (0, 0)),
                      pl.BlockSpec((HEAD_DIM, 1), lambda b, h, p, qt, kt, par: (0, 0))],
            out_specs=pl.BlockSpec((None, t, HEAD_DIM), q_map),
            scratch_shapes=scratch),
        out_shape=jax.ShapeDtypeStruct((batch, length, WIDTH), jnp.float32),
        compiler_params=pltpu.CompilerParams(
            dimension_semantics=("parallel", "parallel", "arbitrary"), vmem_limit_bytes=VMEM_LIMIT),
        name="diff_attn_prompt",
    )(qt, kt, params, z4, k, v, pattern, gain.reshape(HEAD_DIM, 1))


def _diff_attn_cached(params, z4, k, v, gain, *, qpos0, s_valid, lam_init):
    _, batch, t, _ = z4.shape
    s = k.shape[1]
    return pl.pallas_call(
        functools.partial(_attn_cached_kernel, qpos0=qpos0, s_valid=s_valid, lam_init=lam_init),
        grid_spec=pltpu.PrefetchScalarGridSpec(
            num_scalar_prefetch=1,
            grid=(batch, HEADS),
            in_specs=[pl.BlockSpec((None, None, t, HEAD_DIM), lambda b, h, par: (Q_GROUP, b, 0, h)),
                      pl.BlockSpec((None, s, HEAD_DIM), lambda b, h, par: (b, 0, h)),
                      pl.BlockSpec((None, s, HEAD_DIM), lambda b, h, par: (b, 0, h)),
                      pl.BlockSpec((1, HEAD_DIM), lambda b, h, par: (0, 0))],
            out_specs=pl.BlockSpec((None, t, HEAD_DIM), lambda b, h, par: (b, 0, h)),
            scratch_shapes=_attn_scratch(t)),
        out_shape=jax.ShapeDtypeStruct((batch, t, WIDTH), jnp.float32),
        compiler_params=pltpu.CompilerParams(
            dimension_semantics=("parallel", "parallel"), vmem_limit_bytes=VMEM_LIMIT),
        name="diff_attn_cached",
    )(params, z4, k, v, gain)


def _layer_norm(y, g, b):
    mu = jnp.mean(y, axis=1, keepdims=True)
    d = y - mu
    var = jnp.mean(d * d, axis=1, keepdims=True)
    return d * lax.rsqrt(var + LN_EPS) * g + b


def _out_proj_kernel(oa_ref, ob_ref, x_ref, w_ref, g_ref, b_ref, o_ref):
    acc = jnp.dot(_bf(oa_ref[...]), w_ref[0:WIDTH, :], preferred_element_type=jnp.float32)
    acc += jnp.dot(_bf(ob_ref[...]), w_ref[WIDTH:2 * WIDTH, :], preferred_element_type=jnp.float32)
    o_ref[...] = _layer_norm(DEEPNORM_ALPHA * x_ref[...] + acc, g_ref[...], b_ref[...])


def _out_proj(oa, ob, x2d, w_bf16, g, b):
    n = x2d.shape[0]
    tm = 256
    assert n % tm == 0
    row = lambda i: (i, 0)
    const = lambda i: (0, 0)
    return pl.pallas_call(
        _out_proj_kernel,
        grid=(n // tm,),
        in_specs=[pl.BlockSpec((tm, WIDTH), row), pl.BlockSpec((tm, WIDTH), row),
                  pl.BlockSpec((tm, D_MODEL), row), pl.BlockSpec((2 * WIDTH, D_MODEL), const),
                  pl.BlockSpec((1, D_MODEL), const), pl.BlockSpec((1, D_MODEL), const)],
        out_specs=pl.BlockSpec((tm, D_MODEL), row),
        out_shape=jax.ShapeDtypeStruct((n, D_MODEL), jnp.float32),
        compiler_params=pltpu.CompilerParams(dimension_semantics=("parallel",), vmem_limit_bytes=VMEM_LIMIT),
        name="out_proj",
    )(oa, ob, x2d, w_bf16, g, b)


def _top16(vals, pos, ids=None):
    top_v, top_i = [], []
    for _ in range(PEER_TOPK):
        m = jnp.max(vals, axis=0, keepdims=True)
        first = jnp.min(jnp.where(vals == m, pos, jnp.inf), axis=0, keepdims=True)
        hit = pos == first
        top_v.append(m)
        top_i.append(first if ids is None else jnp.max(jnp.where(hit, ids, -1.0), axis=0, keepdims=True))
        vals = jnp.where(hit, -jnp.inf, vals)
    return jnp.concatenate(top_v, axis=0), jnp.concatenate(top_i, axis=0)


def _pair_candidates(sv, si):
    t = sv[0].shape[1]
    row8 = lax.broadcasted_iota(jnp.int32, (ROWS, t), 0).astype(jnp.float32)
    row16 = lax.broadcasted_iota(jnp.int32, (PEER_TOPK, t), 0).astype(jnp.float32)
    vals = [sv[0][0:1] + sv[1]]
    pos = [row16]
    ids = [si[0][0:1] * N_KEYS + si[1]]
    for a in range(1, ROWS):
        n_b = PEER_TOPK // (a + 1)
        vals.append(jnp.where(row8 < n_b, sv[0][a:a + 1] + sv[1][0:ROWS], -jnp.inf))
        pos.append(row8 + float(a * PEER_TOPK))
        ids.append(si[0][a:a + 1] * N_KEYS + si[1][0:ROWS])
    vals.append(sv[0][ROWS:] + sv[1][0:1])
    pos.append((row8 + float(ROWS)) * float(PEER_TOPK))
    ids.append(si[0][ROWS:] * N_KEYS + si[1][0:1])
    return jnp.concatenate(vals, axis=0), jnp.concatenate(pos, axis=0), jnp.concatenate(ids, axis=0)


def _peer_topk_kernel(x_ref, wq_ref, sk_ref, idx_ref, gate_ref):
    q = jnp.dot(_bf(x_ref[...]), wq_ref[...], preferred_element_type=jnp.float32)
    key_id = lax.broadcasted_iota(jnp.int32, (N_KEYS, TOKEN_BLOCK), 0).astype(jnp.float32)
    for h in range(PEER_HEADS):
        sv, si = [], []
        for c in range(2):
            col = (2 * h + c) * HEAD_DIM
            s_t = _dot_nt(sk_ref[2 * h + c], q[:, col:col + HEAD_DIM])
            v16, i16 = _top16(s_t, key_id)
            sv.append(v16)
            si.append(i16)
        cv, eidx = _top16(*_pair_candidates(sv, si))
        e = jnp.exp(cv - cv[0:1])
        rows = slice(h * PEER_TOPK, (h + 1) * PEER_TOPK)
        gate_ref[0, rows, :] = e / jnp.sum(e, axis=0, keepdims=True)
        idx_ref[0, rows, :] = eidx.astype(jnp.int32)


def _peer_topk(x2d, wq_bf16, sk_bf16):
    n = x2d.shape[0]
    nb = n // TOKEN_BLOCK
    return pl.pallas_call(
        _peer_topk_kernel,
        grid=(nb,),
        in_specs=[pl.BlockSpec((TOKEN_BLOCK, D_MODEL), lambda i: (i, 0)),
                  pl.BlockSpec((D_MODEL, D_MODEL), lambda i: (0, 0)),
                  pl.BlockSpec((2 * PEER_HEADS, N_KEYS, HEAD_DIM), lambda i: (0, 0, 0))],
        out_specs=[pl.BlockSpec((1, PEER_SLOTS, TOKEN_BLOCK), lambda i: (i, 0, 0)),
                   pl.BlockSpec((1, PEER_SLOTS, TOKEN_BLOCK), lambda i: (i, 0, 0))],
        out_shape=[jax.ShapeDtypeStruct((nb, PEER_SLOTS, TOKEN_BLOCK), jnp.int32),
                   jax.ShapeDtypeStruct((nb, PEER_SLOTS, TOKEN_BLOCK), jnp.float32)],
        compiler_params=pltpu.CompilerParams(dimension_semantics=("parallel",), vmem_limit_bytes=VMEM_LIMIT),
        name="peer_topk",
    )(x2d, wq_bf16, sk_bf16)


PEER_TB = 32
PEER_RING = 16
PEER_AHEAD = 8


def _pack_tables(u, v):
    ub = lax.bitcast_convert_type(_bf(u), jnp.uint16).astype(jnp.uint32)
    vb = lax.bitcast_convert_type(_bf(v), jnp.uint16).astype(jnp.uint32)
    return ((ub << 16) | vb).reshape(u.shape[0], 1, u.shape[1])


def _peer_mix_kernel(idx_ref, idx_next_ref, gate_ref, x_ref, tab_hbm, g_ref, b_ref, o_ref,
                     buf, xb_ref, y_ref, sem, *, steps):
    i = pl.program_id(0)
    groups = PEER_SLOTS // ROWS
    per_group = PEER_SLOTS // (2 * groups)

    def start_rows(ids_ref, tok, j0, j1):
        slot = tok % PEER_RING
        for j in range(j0, j1):
            e = ids_ref[0, 0, tok * PEER_SLOTS + j]
            pltpu.make_async_copy(tab_hbm.at[e], buf.at[slot, pl.ds(j, 1)], sem.at[slot]).start(priority=j % 2)

    def wait_token(tok):
        slot = tok % PEER_RING
        pltpu.make_async_copy(tab_hbm.at[pl.ds(0, PEER_SLOTS), 0], buf.at[slot], sem.at[slot]).wait()

    def ahead(t):
        nxt = t + PEER_AHEAD
        return (idx_ref, nxt) if nxt < PEER_TB else (idx_next_ref, nxt - PEER_TB)

    def u_phase(t, req):
        wait_token(t)
        parts = []
        for g in range(groups):
            start_rows(*req, per_group * g, per_group * (g + 1))
            r0 = ROWS * g
            acc = None
            for c0 in range(0, D_MODEL, LANES):
                words = buf[t % PEER_RING, r0:r0 + ROWS, c0:c0 + LANES]
                term = (lax.bitcast_convert_type(words & jnp.uint32(0xFFFF0000), jnp.float32)
                        * xb_ref[t, :, c0:c0 + LANES])
                acc = term if acc is None else acc + term
            parts.append(acc)
        act = jnp.sum(jnp.concatenate(parts, axis=0), axis=1, keepdims=True)
        act = 0.5 * act * (1.0 + lax.erf(act * (2.0 ** -0.5)))
        return gate_ref[0, :, t:t + 1] * act

    def v_phase(t, w, req):
        y_acc = jnp.zeros((ROWS, D_MODEL), jnp.float32)
        half = PEER_SLOTS // 2
        for g in range(groups):
            if req is not None:
                start_rows(*req, half + per_group * g, half + per_group * (g + 1))
            r0 = ROWS * g
            words = buf[t % PEER_RING, r0:r0 + ROWS, :]
            y_acc = y_acc + lax.bitcast_convert_type(words << 16, jnp.float32) * w[r0:r0 + ROWS]
        y_ref[t:t + 1, :] = jnp.sum(y_acc, axis=0, keepdims=True)

    @pl.when(i == 0)
    def _():
        for t in range(PEER_AHEAD):
            start_rows(idx_ref, t, 0, PEER_SLOTS)

    for t in range(PEER_TB):
        xb_ref[t] = jnp.broadcast_to(x_ref[t:t + 1, :], (ROWS, D_MODEL))

    w_prev = None
    for t in range(PEER_TB):
        w = u_phase(t, ahead(t))
        if t == 0:
            start_rows(*ahead(0), PEER_SLOTS // 2, PEER_SLOTS)
        else:
            v_phase(t - 1, w_prev, ahead(t))
        w_prev = w
    v_phase(PEER_TB - 1, w_prev, None)

    o_ref[...] = _layer_norm(DEEPNORM_ALPHA * x_ref[...] + y_ref[...], g_ref[...], b_ref[...])

    @pl.when(i == steps - 1)
    def _():
        for t in range(PEER_AHEAD):
            wait_token(t)


def _peer_mix(idx, gate, x2d, table, g, b):
    n = x2d.shape[0]
    steps = n // PEER_TB
    assert n % PEER_TB == 0 and PEER_TB % PEER_RING == 0 and PEER_AHEAD <= PEER_RING - 2
    idx_tm = jnp.swapaxes(idx, 1, 2).reshape(steps, 1, PEER_TB * PEER_SLOTS)
    gate_tm = jnp.swapaxes(jnp.swapaxes(gate, 1, 2).reshape(steps, PEER_TB, PEER_SLOTS), 1, 2)
    ids_block = (1, 1, PEER_TB * PEER_SLOTS)
    return pl.pallas_call(
        functools.partial(_peer_mix_kernel, steps=steps),
        grid=(steps,),
        in_specs=[pl.BlockSpec(ids_block, lambda i: (i, 0, 0), memory_space=pltpu.SMEM),
                  pl.BlockSpec(ids_block, lambda i: (jnp.minimum(i + 1, steps - 1), 0, 0), memory_space=pltpu.SMEM),
                  pl.BlockSpec((1, PEER_SLOTS, PEER_TB), lambda i: (i, 0, 0)),
                  pl.BlockSpec((PEER_TB, D_MODEL), lambda i: (i, 0)),
                  pl.BlockSpec(memory_space=pl.ANY),
                  pl.BlockSpec((1, D_MODEL), lambda i: (0, 0)),
                  pl.BlockSpec((1, D_MODEL), lambda i: (0, 0))],
        out_specs=pl.BlockSpec((PEER_TB, D_MODEL), lambda i: (i, 0)),
        out_shape=jax.ShapeDtypeStruct((n, D_MODEL), jnp.float32),
        scratch_shapes=[pltpu.VMEM((PEER_RING, PEER_SLOTS, D_MODEL), jnp.uint32),
                        pltpu.VMEM((PEER_TB, ROWS, D_MODEL), jnp.float32),
                        pltpu.VMEM((PEER_TB, D_MODEL), jnp.float32),
                        pltpu.SemaphoreType.DMA((PEER_RING,))],
        compiler_params=pltpu.CompilerParams(dimension_semantics=("arbitrary",), vmem_limit_bytes=VMEM_LIMIT),
        name="peer_mix",
    )(idx_tm, idx_tm, gate_tm, x2d, table, g, b)


def _trunk_layer(x, past_k, past_v, s0, lb, attn_params, lam_init, w_in, a_gain, b_gain, w_out,
                 ln1_g, ln1_b, wq, sub_keys, table, ln2_g, ln2_b):
    batch, length, _ = x.shape
    n = batch * length
    x2d = x.reshape(n, D_MODEL)
    z, k2d, v2d = _in_proj(x2d, w_in)
    z4 = z.reshape(N_MIX_GROUPS, batch, length, WIDTH)
    k3d = k2d.reshape(batch, length, WIDTH)
    v3d = v2d.reshape(batch, length, WIDTH)

    if s0 is None:
        s0 = jnp.zeros((batch, HEADS, HEAD_DIM, HEAD_DIM), jnp.float32)
    o_a, s_new = _hgrn(z4, s0, lb, a_gain)

    if past_k is None:
        o_b = _diff_attn_prompt(attn_params, z4, k3d, v3d, b_gain, lam_init=lam_init)
    else:
        past = past_k.shape[1]
        total = past + length
        padded = -(-total // HEAD_DIM) * HEAD_DIM
        pad = jnp.zeros((batch, padded - total, WIDTH), jnp.float32)
        k_all = jnp.concatenate([past_k.reshape(batch, past, WIDTH), k3d, pad], axis=1)
        v_all = jnp.concatenate([past_v.reshape(batch, past, WIDTH), v3d, pad], axis=1)
        o_b = _diff_attn_cached(attn_params, z4, k_all, v_all, b_gain, qpos0=past, s_valid=total, lam_init=lam_init)
    k_new = k2d.reshape(batch, length, HEADS, HEAD_DIM)
    v_new = v2d.reshape(batch, length, HEADS, HEAD_DIM)

    x1 = _out_proj(o_a, o_b.reshape(n, WIDTH), x2d, w_out, ln1_g, ln1_b)
    idx, gate = _peer_topk(x1, wq, sub_keys)
    y = _peer_mix(idx, gate, x1, table, ln2_g, ln2_b)
    return y.reshape(batch, length, D_MODEL), k_new, v_new, s_new


def kernel(x_prompt, x_sample, cache_k, cache_v, state_hgrn, w_in, hgrn_lb, hgrn_norm_g, diff_lq1, diff_lk1,
           diff_lq2, diff_lk2, diff_norm_g, w_out, ln1_g, ln1_b, peer_wq, peer_sub_keys, peer_u, peer_v,
           ln2_g, ln2_b):
    f32 = jnp.float32
    lower_bounds = jnp.cumsum(jax.nn.softmax(hgrn_lb.astype(f32), axis=0), axis=0)
    slopes = 2.0 ** (-8.0 * jnp.arange(1, HEADS + 1, dtype=f32) / HEADS)
    y_p, y_s = x_prompt, x_sample
    outs = [[] for _ in range(6)]
    for l in range(DEPTH):
        lam_init = 0.8 - 0.6 * math.exp(-0.3 * l)
        lam = (jnp.exp(jnp.sum(diff_lq1[l].astype(f32) * diff_lk1[l].astype(f32)))
               - jnp.exp(jnp.sum(diff_lq2[l].astype(f32) * diff_lk2[l].astype(f32))) + lam_init)
        attn_params = jnp.concatenate([lam.reshape(1), slopes])
        shared = (lower_bounds[l].reshape(1, WIDTH), attn_params, lam_init, _bf(w_in[l]),
                  hgrn_norm_g[l].reshape(1, HEAD_DIM), diff_norm_g[l].reshape(1, HEAD_DIM), _bf(w_out[l]),
                  ln1_g[l].reshape(1, D_MODEL), ln1_b[l].reshape(1, D_MODEL), _bf(peer_wq[l]),
                  _bf(peer_sub_keys[l].reshape(2 * PEER_HEADS, N_KEYS, HEAD_DIM)), _pack_tables(peer_u[l], peer_v[l]),
                  ln2_g[l].reshape(1, D_MODEL), ln2_b[l].reshape(1, D_MODEL))
        y_p, kp, vp, sp = _trunk_layer(y_p, None, None, None, *shared)
        y_s, kn, vn, sn = _trunk_layer(y_s, cache_k[l], cache_v[l], state_hgrn[l], *shared)
        for lst, val in zip(outs, (kp, vp, sp, kn, vn, sn)):
            lst.append(val)
    return (y_p, y_s) + tuple(o[0][None] if len(o) == 1 else jnp.stack(o) for o in outs)
```
